```python
import jax
import jax.numpy as jnp
from jax import lax
import numpy as np

D_MODEL = 1024
BATCH = 2
SEQ = 8192
DEPTH = 4
DEC_BATCH = 128
DEC_SEQ = 1
PAST_LEN = 2048
PAGE_SIZE = 128

A_GROUPS = ((128, 1), (512, 4), (2048, 16))
N_GROUPS = 3
A_HEADS = 4
A_HEAD_DIM = 64
A_ROT_DIM = A_HEAD_DIM // 4
ROPE_THETA = 500000.0
A_QKV = N_GROUPS * A_HEADS * A_HEAD_DIM
A_OUT = A_HEADS * A_HEAD_DIM

B_HEADS = 4
B_DK = D_MODEL // (2 * B_HEADS)
B_DV = D_MODEL // B_HEADS
B_KW = B_HEADS * B_DK
B_VW = B_HEADS * B_DV
B_GATE_RANK = 16
B_GATE_TAU = 16.0
B_CHUNK = 64

D_FF = 4 * D_MODEL
EPS = 1e-6

IN_SIZES = (A_QKV, A_QKV, A_QKV, B_KW, B_KW, B_VW, B_GATE_RANK, B_VW, D_MODEL, D_MODEL)
IN_WIDTH = sum(IN_SIZES)

kernel_name = 'gated_dilated_swa_gla_hybrid_step'


def rms_norm(x, g):
    xf = x.astype(jnp.float32)
    y = xf * lax.rsqrt(jnp.mean(xf * xf, axis=-1, keepdims=True) + EPS)
    return (y * g.astype(jnp.float32)).astype(x.dtype)


def rope_partial(x, pos):
    half = A_ROT_DIM // 2
    inv_freq = 1.0 / (ROPE_THETA ** (jnp.arange(half, dtype=jnp.float32) / half))
    ang = pos.astype(jnp.float32)[:, None] * inv_freq[None, :]
    cos = jnp.cos(ang)[:, None, :]
    sin = jnp.sin(ang)[:, None, :]
    xr = x[..., :A_ROT_DIM].astype(jnp.float32)
    x1, x2 = xr[..., :half], xr[..., half:]
    rot = jnp.concatenate([x1 * cos - x2 * sin, x2 * cos + x1 * sin], axis=-1)
    return jnp.concatenate([rot.astype(x.dtype), x[..., A_ROT_DIM:]], axis=-1)


def dilated_attn_prompt(q, k, v, window, dil):
    bsz, t, h, dh = q.shape
    span = window // dil
    n = t // dil
    nb = -(-n // span)
    npad = nb * span

    def to_sub(x):
        return x.reshape(bsz, n, dil, h, dh).transpose(0, 2, 1, 3, 4)

    qs = jnp.pad(to_sub(q), ((0, 0), (0, 0), (0, npad - n), (0, 0), (0, 0)))
    qs = qs.reshape(bsz, dil, nb, span, h, dh)

    def key_blocks(x):
        xs = jnp.pad(to_sub(x), ((0, 0), (0, 0), (span, npad - n), (0, 0), (0, 0)))
        xs = xs.reshape(bsz, dil, nb + 1, span, h, dh)
        return jnp.concatenate([xs[:, :, :-1], xs[:, :, 1:]], axis=3)

    kb = key_blocks(k).astype(jnp.float32)
    vb = key_blocks(v).astype(jnp.float32)
    s = jnp.einsum('brnqhd,brnkhd->brnhqk', qs.astype(jnp.float32), kb) * (dh ** -0.5)
    qi = jnp.arange(span)[:, None]
    kj = jnp.arange(2 * span)[None, :]
    dist = span + qi - kj
    band = (dist >= 0) & (dist <= span)
    kabs = (jnp.arange(nb)[:, None] - 1) * span + jnp.arange(2 * span)[None, :]
    mask = band[None] & (kabs >= 0)[:, None, :]
    s = jnp.where(mask[None, None, :, None], s, -jnp.inf)
    m = jnp.max(s, axis=-1, keepdims=True)
    p = jnp.exp(s - m)
    den = jnp.sum(p, axis=-1)
    o = jnp.einsum('brnhqk,brnkhd->brnqhd', p, vb) / jnp.moveaxis(den, 3, 4)[..., None]
    lse = jnp.moveaxis(m[..., 0] + jnp.log(den), 3, 4)

    def from_sub(x):
        x = x.reshape((bsz, dil, npad) + x.shape[4:])[:, :, :n]
        return jnp.swapaxes(x, 1, 2).reshape((bsz, t) + x.shape[3:])

    return from_sub(o), from_sub(lse)


def dilated_attn_sample(q, k, v, kv_buf, window, dil):
    bsz, s_len, h, dh = q.shape
    buf_len = kv_buf.shape[1]
    keys = jnp.concatenate([kv_buf[:, :, 0].astype(k.dtype), k], axis=1)
    vals = jnp.concatenate([kv_buf[:, :, 1].astype(v.dtype), v], axis=1)
    j = jnp.arange(window // dil + 1)
    idx = buf_len + jnp.arange(s_len)[:, None] - dil * j[None, :]
    valid = idx >= 0
    idx = jnp.maximum(idx, 0)
    kg = keys[:, idx].astype(jnp.float32)
    vg = vals[:, idx].astype(jnp.float32)
    sc = jnp.einsum('bshd,bsjhd->bhsj', q.astype(jnp.float32), kg) * (dh ** -0.5)
    sc = jnp.where(valid[None, None], sc, -jnp.inf)
    m = jnp.max(sc, axis=-1, keepdims=True)
    p = jnp.exp(sc - m)
    den = jnp.sum(p, axis=-1)
    o = jnp.einsum('bhsj,bsjhd->bshd', p, vg) / jnp.swapaxes(den, 1, 2)[..., None]
    lse = jnp.swapaxes(m[..., 0] + jnp.log(den), 1, 2)
    return o, lse


def gla_chunked(q, k, v, log_a, s0):
    bsz, t, h, dk = q.shape
    dv = v.shape[-1]
    c = min(B_CHUNK, t)
    nc = -(-t // c)
    pad = nc * c - t

    def chunks(x):
        x = jnp.pad(x.astype(jnp.float32), ((0, 0), (0, pad), (0, 0), (0, 0)))
        return jnp.moveaxis(x.reshape(bsz, nc, c, h, x.shape[-1]), 1, 0)

    causal = jnp.tril(jnp.ones((c, c), dtype=bool))

    def step(state, inp):
        qc, kc, vc, ac = inp
        b = jnp.cumsum(ac, axis=1)
        o_inter = jnp.einsum('bthk,bhkv->bthv', qc * jnp.exp(b), state)
        diff = b[:, :, None] - b[:, None, :]
        decay = jnp.exp(jnp.where(causal[None, :, :, None, None], diff, -jnp.inf))
        att = jnp.einsum('bthk,btshk,bshk->bhts', qc, decay, kc)
        o_intra = jnp.einsum('bhts,bshv->bthv', att, vc)
        b_last = b[:, -1]
        new_state = jnp.exp(b_last)[..., None] * state + jnp.einsum(
            'bshk,bshv->bhkv', kc * jnp.exp(b_last[:, None] - b), vc)
        return new_state, o_inter + o_intra

    s_fin, o = lax.scan(step, s0.astype(jnp.float32), (chunks(q), chunks(k), chunks(v), chunks(log_a)))
    o = jnp.moveaxis(o, 0, 1).reshape(bsz, nc * c, h, dv)[:, :t]
    return o, s_fin


def hybrid_layer(x, pos, kv_bufs, s0, norm1_g, w_in, w_gate_up, b_gate, gla_norm_g,
                 w_pa, w_pb, w_o, norm2_g, w_ff1, w_ff2):
    bsz, t, _ = x.shape
    h = rms_norm(x, norm1_g)
    proj = h @ w_in
    points = np.cumsum(IN_SIZES)[:-1].tolist()
    qa, ka, va, qb, kb, vb, g_low, r_b, gate_a, gate_b = jnp.split(proj, points, axis=-1)

    heads = N_GROUPS * A_HEADS
    qa = rope_partial(qa.reshape(bsz, t, heads, A_HEAD_DIM), pos)
    ka = rope_partial(ka.reshape(bsz, t, heads, A_HEAD_DIM), pos)
    va = va.reshape(bsz, t, heads, A_HEAD_DIM)
    outs, lses, new_kv = [], [], []
    for g, (window, dil) in enumerate(A_GROUPS):
        sl = slice(g * A_HEADS, (g + 1) * A_HEADS)
        qg, kg, vg = qa[:, :, sl], ka[:, :, sl], va[:, :, sl]
        if kv_bufs is None:
            o, lse = dilated_attn_prompt(qg, kg, vg, window, dil)
            keep = min(window, t)
            new_kv.append(jnp.stack([kg[:, t - keep:], vg[:, t - keep:]], axis=2))
        else:
            o, lse = dilated_attn_sample(qg, kg, vg, kv_bufs[g], window, dil)
            new_kv.append(jnp.stack([kg, vg], axis=2))
        outs.append(o)
        lses.append(lse)
    wts = jax.nn.softmax(jnp.stack(lses), axis=0)
    o_a = jnp.sum(wts[..., None] * jnp.stack(outs), axis=0)
    y_a = o_a.reshape(bsz, t, A_OUT).astype(x.dtype) @ w_pa

    qb = qb.reshape(bsz, t, B_HEADS, B_DK) * (B_DK ** -0.5)
    kb = kb.reshape(bsz, t, B_HEADS, B_DK)
    vb = vb.reshape(bsz, t, B_HEADS, B_DV)
    log_a = jax.nn.log_sigmoid((g_low @ w_gate_up + b_gate).astype(jnp.float32)) / B_GATE_TAU
    log_a = log_a.reshape(bsz, t, B_HEADS, B_DK)
    o_b, s_new = gla_chunked(qb, kb, vb, log_a, s0)
    o_b = rms_norm(o_b, gla_norm_g) * jax.nn.silu(r_b.reshape(bsz, t, B_HEADS, B_DV).astype(jnp.float32))
    y_b = o_b.reshape(bsz, t, B_VW).astype(x.dtype) @ w_pb

    mixed = jax.nn.sigmoid(gate_a) * y_a + jax.nn.sigmoid(gate_b) * y_b
    x = x + mixed @ w_o

    h2 = rms_norm(x, norm2_g)
    x = x + jnp.square(jax.nn.relu(h2 @ w_ff1)) @ w_ff2
    return x, new_kv, s_new


def setup_inputs(seed: int = 0) -> dict:
    key = jax.random.key(seed)
    ks = jax.random.split(key, 20)
    f32 = jnp.float32

    def nrm(k, shape, scale):
        return jax.random.normal(k, shape, f32) * scale

    def kv_buf(k, window):
        return nrm(k, (DEPTH, DEC_BATCH, min(window, PAST_LEN), 2, A_HEADS, A_HEAD_DIM), 1.0)

    return {
        'x_prompt': nrm(ks[0], (BATCH, SEQ, D_MODEL), 1.0),
        'x_sample': nrm(ks[1], (DEC_BATCH, DEC_SEQ, D_MODEL), 1.0),
        'cache_a1_kv': kv_buf(ks[2], A_GROUPS[0][0]),
        'cache_a2_kv': kv_buf(ks[3], A_GROUPS[1][0]),
        'cache_a3_kv': kv_buf(ks[4], A_GROUPS[2][0]),
        'state_gla': nrm(ks[5], (DEPTH, DEC_BATCH, B_HEADS, B_DK, B_DV), 0.5),
        'norm1_g': 1.0 + nrm(ks[6], (DEPTH, D_MODEL), 0.02),
        'w_in': nrm(ks[7], (DEPTH, D_MODEL, IN_WIDTH), D_MODEL ** -0.5),
        'w_gate_up': nrm(ks[8], (DEPTH, B_GATE_RANK, B_KW), B_GATE_RANK ** -0.5),
        'b_gate': nrm(ks[9], (DEPTH, B_KW), 0.1),
        'gla_norm_g': 1.0 + nrm(ks[10], (DEPTH, B_DV), 0.02),
        'w_pa': nrm(ks[11], (DEPTH, A_OUT, D_MODEL), A_OUT ** -0.5),
        'w_pb': nrm(ks[12], (DEPTH, B_VW, D_MODEL), B_VW ** -0.5),
        'w_o': nrm(ks[13], (DEPTH, D_MODEL, D_MODEL), D_MODEL ** -0.5),
        'norm2_g': 1.0 + nrm(ks[14], (DEPTH, D_MODEL), 0.02),
        'w_ff1': nrm(ks[15], (DEPTH, D_MODEL, D_FF), D_MODEL ** -0.5),
        'w_ff2': nrm(ks[16], (DEPTH, D_FF, D_MODEL), D_FF ** -0.5),
        'final_norm_g': 1.0 + nrm(ks[17], (D_MODEL,), 0.02),
    }


def reference(x_prompt, x_sample, cache_a1_kv, cache_a2_kv, cache_a3_kv, state_gla,
              norm1_g, w_in, w_gate_up, b_gate, gla_norm_g, w_pa, w_pb, w_o, norm2_g,
              w_ff1, w_ff2, final_norm_g):
    pos_p = jnp.arange(x_prompt.shape[1], dtype=jnp.int32)
    pos_s = PAST_LEN + jnp.arange(x_sample.shape[1], dtype=jnp.int32)
    s0_p = jnp.zeros((x_prompt.shape[0], B_HEADS, B_DK, B_DV), jnp.float32)
    xp, xs = x_prompt, x_sample
    p_kv = ([], [], [])
    s_kv = ([], [], [])
    p_st, s_st = [], []
    for l in range(DEPTH):
        lw = (norm1_g[l], w_in[l], w_gate_up[l], b_gate[l], gla_norm_g[l], w_pa[l], w_pb[l],
              w_o[l], norm2_g[l], w_ff1[l], w_ff2[l])
        xp, kv_p, st_p = hybrid_layer(xp, pos_p, None, s0_p, *lw)
        xs, kv_s, st_s = hybrid_layer(
            xs, pos_s, (cache_a1_kv[l], cache_a2_kv[l], cache_a3_kv[l]), state_gla[l], *lw)
        for g in range(N_GROUPS):
            p_kv[g].append(kv_p[g])
            s_kv[g].append(kv_s[g])
        p_st.append(st_p)
        s_st.append(st_s)
    y_prompt = rms_norm(xp, final_norm_g)
    y_sample = rms_norm(xs, final_norm_g)
    return (y_prompt, y_sample,
            jnp.stack(p_kv[0]), jnp.stack(p_kv[1]), jnp.stack(p_kv[2]), jnp.stack(p_st),
            jnp.stack(s_kv[0]), jnp.stack(s_kv[1]), jnp.stack(s_kv[2]), jnp.stack(s_st))
```

```python
import functools

import jax
import jax.numpy as jnp
from jax import lax
from jax.experimental import pallas as pl
from jax.experimental.pallas import tpu as pltpu

F32 = jnp.float32
BF16 = jnp.bfloat16

D_MODEL = 1024
PAST_LEN = 2048
A_GROUPS = ((128, 1), (512, 4), (2048, 16))
N_GROUPS = 3
A_HEADS = 4
A_HEAD_DIM = 64
A_ROT_DIM = A_HEAD_DIM // 4
ROPE_THETA = 500000.0
A_QKV = N_GROUPS * A_HEADS * A_HEAD_DIM
A_OUT = A_HEADS * A_HEAD_DIM
A_SPAN = 128

B_HEADS = 4
B_DK = 128
B_DV = 256
B_KW = B_HEADS * B_DK
B_VW = B_HEADS * B_DV
B_GATE_RANK = 16
B_GATE_TAU = 16.0
D_FF = 4 * D_MODEL
EPS = 1e-6

LANES = 128
GLA_CHUNK = 128
GLA_SUB = 16
NEG_BIG = -1e30
VMEM_LIMIT = 56 * 1024 * 1024

GL_PAD = LANES
_SEG_SIZES = (A_QKV, A_QKV, A_QKV, B_KW, B_KW, B_VW, GL_PAD, B_VW, D_MODEL, D_MODEL)
_SEG_NAMES = ("qa", "ka", "va", "qb", "kb", "vb", "gl", "rb", "ga", "gb")
_SEG = {}
_off = 0
for _n, _s in zip(_SEG_NAMES, _SEG_SIZES):
    _SEG[_n] = (_off, _off + _s)
    _off += _s
IN_WIDTH_PAD = _off

NT_DIMS = (((1,), (1,)), ((), ()))
TN_DIMS = (((0,), (0,)), ((), ()))


def _params(sem):
    return pltpu.CompilerParams(dimension_semantics=sem, vmem_limit_bytes=VMEM_LIMIT)


def _sigmoid(x):
    return 1.0 / (1.0 + jnp.exp(-x))


def _log_sigmoid(x):
    return jnp.minimum(x, 0.0) - jnp.log(1.0 + jnp.exp(-jnp.abs(x)))


def _rms(x, g):
    ms = jnp.mean(x * x, axis=-1, keepdims=True)
    return x * lax.rsqrt(ms + EPS) * g


def _inproj_kernel(x_ref, g_ref, w_ref, cos_ref, sa_ref, sb_ref,
                   qa_ref, ka_ref, va_ref, qb_ref, kb_ref, vb_ref, gl_ref, rb_ref, ga_ref, gb_ref):
    h = _rms(x_ref[...], g_ref[...]).astype(BF16)

    def seg(name):
        c0, c1 = _SEG[name]
        return jnp.dot(h, w_ref[:, c0:c1], preferred_element_type=F32)

    cosl, sa, sb = cos_ref[...], sa_ref[...], sb_ref[...]

    def rope_store(acc, out_ref, scale):
        for c in range(A_QKV // LANES):
            xc = acc[:, c * LANES:(c + 1) * LANES]
            r = xc * cosl + pltpu.roll(xc, LANES - A_ROT_DIM // 2, 1) * sa + pltpu.roll(xc, A_ROT_DIM // 2, 1) * sb
            if scale != 1.0:
                r = r * scale
            out_ref[:, c * LANES:(c + 1) * LANES] = r.astype(out_ref.dtype)

    rope_store(seg("qa"), qa_ref, A_HEAD_DIM ** -0.5)
    rope_store(seg("ka"), ka_ref, 1.0)
    va_ref[...] = seg("va").astype(va_ref.dtype)
    qb_ref[...] = (seg("qb") * (B_DK ** -0.5)).astype(qb_ref.dtype)
    kb_ref[...] = seg("kb").astype(kb_ref.dtype)
    vb_ref[...] = seg("vb").astype(vb_ref.dtype)
    gl_ref[...] = seg("gl")
    r = seg("rb")
    rb_ref[...] = (r * _sigmoid(r)).astype(rb_ref.dtype)
    ga_ref[...] = _sigmoid(seg("ga")).astype(ga_ref.dtype)
    gb_ref[...] = _sigmoid(seg("gb")).astype(gb_ref.dtype)


def _inproj(x, g1, w, cosl, sa, sb, tm, table_blocks):
    n = x.shape[0]
    row = lambda i: (i, 0)
    const = lambda i: (0, 0)
    tab = lambda i: (i % table_blocks, 0)
    widths = {"qa": A_QKV, "ka": A_QKV, "va": A_QKV, "qb": B_KW, "kb": B_KW, "vb": B_VW,
              "gl": GL_PAD, "rb": B_VW, "ga": D_MODEL, "gb": D_MODEL}
    out_shape = [jax.ShapeDtypeStruct((n, widths[k]), F32 if k == "gl" else BF16) for k in _SEG_NAMES]
    out_specs = [pl.BlockSpec((tm, widths[k]), row) for k in _SEG_NAMES]
    return pl.pallas_call(
        _inproj_kernel,
        grid=(n // tm,),
        in_specs=[
            pl.BlockSpec((tm, D_MODEL), row),
            pl.BlockSpec((1, D_MODEL), const),
            pl.BlockSpec((D_MODEL, IN_WIDTH_PAD), const, pipeline_mode=pl.Buffered(1)),
            pl.BlockSpec((tm, LANES), tab),
            pl.BlockSpec((tm, LANES), tab),
            pl.BlockSpec((tm, LANES), tab),
        ],
        out_specs=out_specs,
        out_shape=out_shape,
        compiler_params=_params(("parallel",)),
        name="inproj",
    )(x, g1, w, cosl, sa, sb)


def _attn_kernel(q_ref, kp_ref, kc_ref, vp_ref, vc_ref, o_ref, lse_ref, kbuf, vbuf, *, tq):
    i = pl.program_id(2)
    kbuf[0:A_SPAN, :] = kp_ref[0]
    kbuf[A_SPAN:, :] = kc_ref[0]
    vbuf[0:A_SPAN, :] = vp_ref[0]
    vbuf[A_SPAN:, :] = vc_ref[0]
    row = lax.broadcasted_iota(jnp.int32, (A_SPAN, 2 * A_SPAN), 0)
    col = lax.broadcasted_iota(jnp.int32, (A_SPAN, 2 * A_SPAN), 1)
    band = (col >= row) & (col <= row + A_SPAN)
    lane = lax.broadcasted_iota(jnp.int32, (A_SPAN, LANES), 1)
    low = lane < A_HEAD_DIM

    def body(j, carry):
        r0 = pl.multiple_of(j * A_SPAN, A_SPAN)
        thresh = jnp.where(i * tq + j * A_SPAN == 0, A_SPAN, 0)
        mask = band & (col >= thresh)
        for hp in range(A_HEADS // 2):
            cs = slice(hp * LANES, (hp + 1) * LANES)
            qp = q_ref[0, pl.ds(r0, A_SPAN), cs]
            kw = kbuf[pl.ds(r0, 2 * A_SPAN), cs]
            vw = vbuf[pl.ds(r0, 2 * A_SPAN), cs]
            res = []
            for hh in range(2):
                qm = jnp.where(low if hh == 0 else jnp.logical_not(low), qp, jnp.zeros_like(qp))
                s = lax.dot_general(qm, kw, NT_DIMS, preferred_element_type=F32)
                s = jnp.where(mask, s, NEG_BIG)
                m = jnp.max(s, axis=1, keepdims=True)
                p = jnp.exp(s - m)
                l = jnp.sum(p, axis=1, keepdims=True)
                pv = jnp.dot(p.astype(BF16), vw, preferred_element_type=F32)
                res.append((pv / l, m + jnp.log(l)))
            o_ref[0, pl.ds(r0, A_SPAN), cs] = jnp.where(low, res[0][0], res[1][0]).astype(o_ref.dtype)
            lse_ref[0, pl.ds(r0, A_SPAN), cs] = jnp.where(low, res[0][1], res[1][1])
        return carry

    lax.fori_loop(0, tq // A_SPAN, body, 0)


def _attn_group(qa, ka, va, bsz, t, g, tq_max=1024):
    dil = A_GROUPS[g][1]
    n = t // dil
    tq = min(tq_max, n)
    ncol_in = A_QKV // A_OUT
    qv = qa.reshape(bsz, n, dil * A_QKV)
    kv = ka.reshape(bsz, n, dil * A_QKV)
    vv = va.reshape(bsz, n, dil * A_QKV)
    sub = tq // A_SPAN
    cur = lambda b, r, i: (b, i, r * ncol_in + g)
    prev = lambda b, r, i: (b, jnp.maximum(i * sub - 1, 0), r * ncol_in + g)
    out = lambda b, r, i: (b, i, r)
    o, lse = pl.pallas_call(
        functools.partial(_attn_kernel, tq=tq),
        grid=(bsz, dil, n // tq),
        in_specs=[
            pl.BlockSpec((1, tq, A_OUT), cur),
            pl.BlockSpec((1, A_SPAN, A_OUT), prev),
            pl.BlockSpec((1, tq, A_OUT), cur),
            pl.BlockSpec((1, A_SPAN, A_OUT), prev),
            pl.BlockSpec((1, tq, A_OUT), cur),
        ],
        out_specs=[pl.BlockSpec((1, tq, A_OUT), out), pl.BlockSpec((1, tq, A_OUT), out)],
        out_shape=[jax.ShapeDtypeStruct((bsz, n, dil * A_OUT), BF16),
                   jax.ShapeDtypeStruct((bsz, n, dil * A_OUT), F32)],
        scratch_shapes=[pltpu.VMEM((tq + A_SPAN, A_OUT), BF16), pltpu.VMEM((tq + A_SPAN, A_OUT), BF16)],
        compiler_params=_params(("parallel", "parallel", "parallel")),
        name=f"attn_g{g}",
    )(qv, kv, kv, vv, vv)
    return o.reshape(bsz * t, A_OUT), lse.reshape(bsz * t, A_OUT)


def _gate_log_decay(gl, wgu, bg):
    x = jnp.dot(gl, wgu, preferred_element_type=F32, precision=lax.Precision.HIGHEST) + bg
    return _log_sigmoid(x) * (1.0 / B_GATE_TAU)


def _gla_kernel(q_ref, k_ref, v_ref, gl_ref, r_ref, wgu_ref, bg_ref, gn_ref, o_ref, sfin_ref, st_ref):
    c = pl.program_id(1)
    C = GLA_CHUNK
    nsub = C // GLA_SUB

    @pl.when(c == 0)
    def _():
        st_ref[...] = jnp.zeros_like(st_ref)

    la = _gate_log_decay(gl_ref[0], wgu_ref[...], bg_ref[...])
    ri = lax.broadcasted_iota(jnp.int32, (C, C), 0)
    ci = lax.broadcasted_iota(jnp.int32, (C, C), 1)
    tril = ci <= ri
    b_all = jnp.dot(tril.astype(F32), la, preferred_element_type=F32, precision=lax.Precision.HIGHEST)
    srow = lax.broadcasted_iota(jnp.int32, (C, B_DK), 0)
    gn = gn_ref[...]

    for h in range(B_HEADS):
        ks = slice(h * B_DK, (h + 1) * B_DK)
        vs = slice(h * B_DV, (h + 1) * B_DV)
        b = b_all[:, ks]
        q = q_ref[0, :, ks].astype(F32)
        k = k_ref[0, :, ks].astype(F32)
        v = v_ref[0, :, vs]
        st = st_ref[h]
        o_inter = lax.dot_general((q * jnp.exp(b)).astype(BF16), st.astype(BF16), NT_DIMS,
                                  preferred_element_type=F32)
        beta = b.reshape(nsub, GLA_SUB, B_DK)[:, 0:1, :]
        beta_rows = jnp.broadcast_to(beta, (nsub, GLA_SUB, B_DK)).reshape(C, B_DK)
        qt = (q * jnp.exp(b - beta_rows)).astype(BF16)
        kts = []
        for i in range(nsub):
            bi = b[GLA_SUB * i:GLA_SUB * i + 1, :]
            e = jnp.exp(jnp.where(srow < GLA_SUB * (i + 1), bi - b, NEG_BIG))
            kts.append((k * e).astype(BF16))
        att_all = lax.dot_general(qt, jnp.concatenate(kts, axis=0), NT_DIMS, preferred_element_type=F32)
        att = jnp.concatenate(
            [att_all[GLA_SUB * i:GLA_SUB * (i + 1), i * C:(i + 1) * C] for i in range(nsub)], axis=0)
        att = jnp.where(tril, att, 0.0).astype(BF16)
        o = o_inter + jnp.dot(att, v, preferred_element_type=F32)
        bl = b[C - 1:C, :]
        kh = (k * jnp.exp(bl - b)).astype(BF16)
        st_ref[h] = st * jnp.exp(bl) + lax.dot_general(v, kh, TN_DIMS, preferred_element_type=F32)
        y = _rms(o, gn) * r_ref[0, :, vs].astype(F32)
        o_ref[0, :, vs] = y.astype(o_ref.dtype)

    @pl.when(c == pl.num_programs(1) - 1)
    def _():
        for h in range(B_HEADS):
            sfin_ref[0, h] = st_ref[h].T


def _gla_prompt(qb, kb, vb, gl, rb, wgu, bg, gn, bsz, t):
    C = GLA_CHUNK
    blk = lambda b, c: (b, c, 0)
    const = lambda b, c: (0, 0)
    o, sfin = pl.pallas_call(
        _gla_kernel,
        grid=(bsz, t // C),
        in_specs=[
            pl.BlockSpec((1, C, B_KW), blk),
            pl.BlockSpec((1, C, B_KW), blk),
            pl.BlockSpec((1, C, B_VW), blk),
            pl.BlockSpec((1, C, GL_PAD), blk),
            pl.BlockSpec((1, C, B_VW), blk),
            pl.BlockSpec((GL_PAD, B_KW), const),
            pl.BlockSpec((1, B_KW), const),
            pl.BlockSpec((1, B_DV), const),
        ],
        out_specs=[pl.BlockSpec((1, C, B_VW), blk),
                   pl.BlockSpec((1, B_HEADS, B_DK, B_DV), lambda b, c: (b, 0, 0, 0))],
        out_shape=[jax.ShapeDtypeStruct((bsz, t, B_VW), BF16),
                   jax.ShapeDtypeStruct((bsz, B_HEADS, B_DK, B_DV), F32)],
        scratch_shapes=[pltpu.VMEM((B_HEADS, B_DV, B_DK), F32)],
        compiler_params=_params(("parallel", "arbitrary")),
        name="gla_prompt",
    )(qb.reshape(bsz, t, B_KW), kb.reshape(bsz, t, B_KW), vb.reshape(bsz, t, B_VW),
      gl.reshape(bsz, t, GL_PAD), rb.reshape(bsz, t, B_VW), wgu, bg, gn)
    return o.reshape(bsz * t, B_VW), sfin


def _merge_tail(oa, ob_ref, ga_ref, gb_ref, x_ref, wpa_ref, wpb_ref, wo_ref, g2_ref, xo_ref, h2_ref):
    ya = jnp.dot(oa.astype(BF16), wpa_ref[...], preferred_element_type=F32)
    yb = jnp.dot(ob_ref[...].astype(BF16), wpb_ref[...], preferred_element_type=F32)
    mixed = ga_ref[...].astype(F32) * ya + gb_ref[...].astype(F32) * yb
    xn = x_ref[...] + jnp.dot(mixed.astype(BF16), wo_ref[...], preferred_element_type=F32)
    xo_ref[...] = xn
    h2_ref[...] = _rms(xn, g2_ref[...]).astype(h2_ref.dtype)


def _merge_kernel_groups(o1_ref, o2_ref, o3_ref, l1_ref, l2_ref, l3_ref, *rest):
    l1, l2, l3 = l1_ref[...], l2_ref[...], l3_ref[...]
    m = jnp.maximum(jnp.maximum(l1, l2), l3)
    e1, e2, e3 = jnp.exp(l1 - m), jnp.exp(l2 - m), jnp.exp(l3 - m)
    num = e1 * o1_ref[...].astype(F32) + e2 * o2_ref[...].astype(F32) + e3 * o3_ref[...].astype(F32)
    _merge_tail(num / (e1 + e2 + e3), *rest)


def _merge_kernel_single(oa_ref, *rest):
    _merge_tail(oa_ref[...], *rest)


def _merge(attn_inputs, ob, ga, gb, x, wpa, wpb, wo, g2, tm):
    n = x.shape[0]
    row = lambda i: (i, 0)
    const = lambda i: (0, 0)
    kern = _merge_kernel_groups if len(attn_inputs) == 2 * N_GROUPS else _merge_kernel_single
    return pl.pallas_call(
        kern,
        grid=(n // tm,),
        in_specs=[pl.BlockSpec((tm, A_OUT), row) for _ in attn_inputs] + [
            pl.BlockSpec((tm, B_VW), row),
            pl.BlockSpec((tm, D_MODEL), row),
            pl.BlockSpec((tm, D_MODEL), row),
            pl.BlockSpec((tm, D_MODEL), row),
            pl.BlockSpec((A_OUT, D_MODEL), const),
            pl.BlockSpec((B_VW, D_MODEL), const),
            pl.BlockSpec((D_MODEL, D_MODEL), const),
            pl.BlockSpec((1, D_MODEL), const),
        ],
        out_specs=[pl.BlockSpec((tm, D_MODEL), row), pl.BlockSpec((tm, D_MODEL), row)],
        out_shape=[jax.ShapeDtypeStruct((n, D_MODEL), F32), jax.ShapeDtypeStruct((n, D_MODEL), BF16)],
        compiler_params=_params(("parallel",)),
        name="merge",
    )(*attn_inputs, ob, ga, gb, x, wpa, wpb, wo, g2)


def _ffn_kernel(h_ref, x_ref, w1_ref, w2_ref, o_ref):
    @pl.when(pl.program_id(1) == 0)
    def _():
        o_ref[...] = x_ref[...]

    a = jnp.maximum(jnp.dot(h_ref[...], w1_ref[...], preferred_element_type=F32), 0.0)
    o_ref[...] += jnp.dot((a * a).astype(BF16), w2_ref[...], preferred_element_type=F32)


def _ffn(h2, x, w1, w2, tm, tf=1024):
    n = x.shape[0]
    return pl.pallas_call(
        _ffn_kernel,
        grid=(n // tm, D_FF // tf),
        in_specs=[
            pl.BlockSpec((tm, D_MODEL), lambda i, k: (i, 0)),
            pl.BlockSpec((tm, D_MODEL), lambda i, k: (i, 0)),
            pl.BlockSpec((D_MODEL, tf), lambda i, k: (0, k)),
            pl.BlockSpec((tf, D_MODEL), lambda i, k: (k, 0)),
        ],
        out_specs=pl.BlockSpec((tm, D_MODEL), lambda i, k: (i, 0)),
        out_shape=jax.ShapeDtypeStruct((n, D_MODEL), F32),
        compiler_params=_params(("parallel", "arbitrary")),
        name="ffn",
    )(h2, x, w1, w2)


def _final_norm_kernel(x_ref, g_ref, o_ref):
    o_ref[...] = _rms(x_ref[...], g_ref[...])


def _final_norm(x, g, tm):
    n = x.shape[0]
    return pl.pallas_call(
        _final_norm_kernel,
        grid=(n // tm,),
        in_specs=[pl.BlockSpec((tm, D_MODEL), lambda i: (i, 0)), pl.BlockSpec((1, D_MODEL), lambda i: (0, 0))],
        out_specs=pl.BlockSpec((tm, D_MODEL), lambda i: (i, 0)),
        out_shape=jax.ShapeDtypeStruct((n, D_MODEL), F32),
        compiler_params=_params(("parallel",)),
        name="final_norm",
    )(x, g)


def _sattn_kernel(q_ref, k_ref, v_ref, c1_ref, c2_ref, c3_ref, o_ref, *, bt):
    kv_w = A_OUT
    head_shift = A_HEAD_DIM.bit_length() - 1
    si = lax.broadcasted_iota(jnp.int32, (kv_w, kv_w), 0) >> head_shift
    sj = lax.broadcasted_iota(jnp.int32, (kv_w, kv_w), 1) >> head_shift
    seg = jnp.where(si == sj, 1.0, 0.0).astype(BF16)
    qs, kn, vn, sn = [], [], [], []
    for g in range(N_GROUPS):
        gs = slice(g * A_OUT, (g + 1) * A_OUT)
        qs.append(q_ref[:, gs].astype(F32))
        kn.append(k_ref[:, gs].astype(F32))
        vn.append(v_ref[:, gs].astype(F32))
        sn.append(jnp.dot((qs[g] * kn[g]).astype(BF16), seg, preferred_element_type=F32))
    for b in range(bt):
        outs, lses = [], []
        for g, c_ref in enumerate((c1_ref, c2_ref, c3_ref)):
            kc = c_ref[0, b, :, 0:kv_w]
            vc = c_ref[0, b, :, kv_w:2 * kv_w]
            qb = qs[g][b:b + 1, :]
            s = jnp.dot((kc * qb).astype(BF16), seg, preferred_element_type=F32)
            snb = sn[g][b:b + 1, :]
            m = jnp.maximum(jnp.max(s, axis=0, keepdims=True), snb)
            p = jnp.exp(s - m)
            pn = jnp.exp(snb - m)
            l = jnp.sum(p, axis=0, keepdims=True) + pn
            o = (jnp.sum(p * vc, axis=0, keepdims=True) + pn * vn[g][b:b + 1, :]) / l
            outs.append(o)
            lses.append(m + jnp.log(l))
        m = jnp.maximum(jnp.maximum(lses[0], lses[1]), lses[2])
        es = [jnp.exp(x - m) for x in lses]
        o_ref[b:b + 1, :] = (es[0] * outs[0] + es[1] * outs[1] + es[2] * outs[2]) / (es[0] + es[1] + es[2])


def _sattn(qa, ka, va, caches, layer, bt=8):
    nb = qa.shape[0]
    views = []
    for c, (window, dil) in zip(caches, A_GROUPS):
        depth, _, buf_len = c.shape[:3]
        views.append(c.reshape(depth, nb, buf_len // dil, dil * 2 * A_OUT))
    row = lambda i: (i, 0)
    cspec = pl.BlockSpec((1, bt, A_SPAN, 2 * A_OUT), lambda i: (layer, i, 0, 0))
    return pl.pallas_call(
        functools.partial(_sattn_kernel, bt=bt),
        grid=(nb // bt,),
        in_specs=[pl.BlockSpec((bt, A_QKV), row)] * 3 + [cspec] * 3,
        out_specs=pl.BlockSpec((bt, A_OUT), row),
        out_shape=jax.ShapeDtypeStruct((nb, A_OUT), F32),
        compiler_params=_params(("parallel",)),
        name="attn_sample",
    )(qa, ka, va, *views)


def _sgla_kernel(q_ref, k_ref, v_ref, gl_ref, r_ref, wgu_ref, bg_ref, gn_ref, s_ref, o_ref, so_ref, oraw, *, bt):
    a = jnp.exp(_gate_log_decay(gl_ref[...], wgu_ref[...], bg_ref[...]))
    q = q_ref[...].astype(F32)
    k = k_ref[...].astype(F32)
    v = v_ref[...].astype(F32)

    def column(arr, b, h):
        r = arr[b:b + 1, h * B_DK:(h + 1) * B_DK]
        return jnp.broadcast_to(r, (LANES, B_DK)).T

    for b in range(bt):
        for h in range(B_HEADS):
            ac, kc, qc = column(a, b, h), column(k, b, h), column(q, b, h)
            for half in range(B_DV // LANES):
                ls = slice(half * LANES, (half + 1) * LANES)
                vh = v[b:b + 1, h * B_DV + half * LANES:h * B_DV + (half + 1) * LANES]
                sn = ac * s_ref[0, b, h, :, ls] + kc * vh
                so_ref[b, h, :, ls] = sn
                oraw[b:b + 1, h * B_DV + half * LANES:h * B_DV + (half + 1) * LANES] = jnp.sum(
                    qc * sn, axis=0, keepdims=True)
    gn = gn_ref[...]
    for h in range(B_HEADS):
        vs = slice(h * B_DV, (h + 1) * B_DV)
        o_ref[:, vs] = _rms(oraw[:, vs], gn) * r_ref[:, vs].astype(F32)


def _sgla(qb, kb, vb, gl, rb, wgu, bg, gn, state, layer, bt=8):
    nb = qb.shape[0]
    row = lambda i: (i, 0)
    const = lambda i: (0, 0)
    sblk = (1, bt, B_HEADS, B_DK, B_DV)
    return pl.pallas_call(
        functools.partial(_sgla_kernel, bt=bt),
        grid=(nb // bt,),
        in_specs=[
            pl.BlockSpec((bt, B_KW), row),
            pl.BlockSpec((bt, B_KW), row),
            pl.BlockSpec((bt, B_VW), row),
            pl.BlockSpec((bt, GL_PAD), row),
            pl.BlockSpec((bt, B_VW), row),
            pl.BlockSpec((GL_PAD, B_KW), const),
            pl.BlockSpec((1, B_KW), const),
            pl.BlockSpec((1, B_DV), const),
            pl.BlockSpec(sblk, lambda i: (layer, i, 0, 0, 0)),
        ],
        out_specs=[pl.BlockSpec((bt, B_VW), row), pl.BlockSpec(sblk[1:], lambda i: (i, 0, 0, 0))],
        out_shape=[jax.ShapeDtypeStruct((nb, B_VW), F32),
                   jax.ShapeDtypeStruct((nb, B_HEADS, B_DK, B_DV), F32)],
        scratch_shapes=[pltpu.VMEM((bt, B_VW), F32)],
        compiler_params=_params(("parallel",)),
        name="gla_sample",
    )(qb, kb, vb, gl, rb, wgu, bg, gn, state)


def _rope_tables(pos):
    half = A_ROT_DIM // 2
    inv_freq = 1.0 / (ROPE_THETA ** (jnp.arange(half, dtype=F32) / half))
    ang = pos.astype(F32)[:, None] * inv_freq[None, :]
    cos, sin = jnp.cos(ang), jnp.sin(ang)
    n = pos.shape[0]
    pad = A_HEAD_DIM - A_ROT_DIM
    cos_h = jnp.concatenate([cos, cos, jnp.ones((n, pad), F32)], axis=1)
    sa_h = jnp.concatenate([-sin, jnp.zeros((n, half + pad), F32)], axis=1)
    sb_h = jnp.concatenate([jnp.zeros((n, half), F32), sin, jnp.zeros((n, pad), F32)], axis=1)
    rep = LANES // A_HEAD_DIM
    return jnp.tile(cos_h, (1, rep)), jnp.tile(sa_h, (1, rep)), jnp.tile(sb_h, (1, rep))


def _kv_rows(ka, va, g):
    gs = slice(g * A_OUT, (g + 1) * A_OUT)
    k = ka[..., gs].astype(F32).reshape(ka.shape[:-1] + (A_HEADS, A_HEAD_DIM))
    v = va[..., gs].astype(F32).reshape(va.shape[:-1] + (A_HEADS, A_HEAD_DIM))
    return jnp.stack([k, v], axis=-3)


def kernel(x_prompt, x_sample, cache_a1_kv, cache_a2_kv, cache_a3_kv, state_gla, norm1_g, w_in, w_gate_up, b_gate,
           gla_norm_g, w_pa, w_pb, w_o, norm2_g, w_ff1, w_ff2, final_norm_g):
    bsz, t, _ = x_prompt.shape
    nb = x_sample.shape[0]
    depth = w_in.shape[0]
    assert x_sample.shape[1] == 1 and t % (A_GROUPS[-1][1] * A_SPAN) == 0 and t % GLA_CHUNK == 0
    caches = (cache_a1_kv, cache_a2_kv, cache_a3_kv)
    for c, (window, _) in zip(caches, A_GROUPS):
        assert c.shape[2] == window

    gl0, gl1 = _SEG["gl"][0], _SEG["gl"][0] + B_GATE_RANK
    w_in_p = jnp.concatenate(
        [w_in[:, :, :gl0], w_in[:, :, gl0:gl1], jnp.zeros((depth, D_MODEL, GL_PAD - B_GATE_RANK), w_in.dtype),
         w_in[:, :, gl1:]], axis=2).astype(BF16)
    wgu_p = jnp.concatenate([w_gate_up, jnp.zeros((depth, GL_PAD - B_GATE_RANK, B_KW), F32)], axis=1)
    w_pa_b, w_pb_b, w_o_b = w_pa.astype(BF16), w_pb.astype(BF16), w_o.astype(BF16)
    w_ff1_b, w_ff2_b = w_ff1.astype(BF16), w_ff2.astype(BF16)

    tm_p = 512
    tabs_p = _rope_tables(jnp.arange(t, dtype=jnp.int32))
    tabs_s = _rope_tables(jnp.full((nb,), PAST_LEN, dtype=jnp.int32))

    xp = x_prompt.reshape(bsz * t, D_MODEL)
    xs = x_sample.reshape(nb, D_MODEL)
    p_kv = [[], [], []]
    s_kv = [[], [], []]
    p_st, s_st = [], []
    for l in range(depth):
        g1 = norm1_g[l][None]
        g2 = norm2_g[l][None]
        bg = b_gate[l][None]
        gn = gla_norm_g[l][None]

        qa, ka, va, qb, kb, vb, gl, rb, ga, gb = _inproj(xp, g1, w_in_p[l], *tabs_p, tm_p, t // tm_p)
        attn = [_attn_group(qa, ka, va, bsz, t, g) for g in range(N_GROUPS)]
        ob, st_p = _gla_prompt(qb, kb, vb, gl, rb, wgu_p[l], bg, gn, bsz, t)
        xp, h2 = _merge([a[0] for a in attn] + [a[1] for a in attn], ob, ga, gb, xp,
                        w_pa_b[l], w_pb_b[l], w_o_b[l], g2, tm_p)
        xp = _ffn(h2, xp, w_ff1_b[l], w_ff2_b[l], 1024)
        ka3 = ka.reshape(bsz, t, A_QKV)
        va3 = va.reshape(bsz, t, A_QKV)
        for g, (window, _) in enumerate(A_GROUPS):
            keep = min(window, t)
            p_kv[g].append(_kv_rows(ka3[:, t - keep:], va3[:, t - keep:], g))
        p_st.append(st_p)

        qa, ka, va, qb, kb, vb, gl, rb, ga, gb = _inproj(xs, g1, w_in_p[l], *tabs_s, nb, 1)
        oa = _sattn(qa, ka, va, caches, l)
        ob, st_s = _sgla(qb, kb, vb, gl, rb, wgu_p[l], bg, gn, state_gla, l)
        xs, h2 = _merge([oa], ob, ga, gb, xs, w_pa_b[l], w_pb_b[l], w_o_b[l], g2, nb)
        xs = _ffn(h2, xs, w_ff1_b[l], w_ff2_b[l], nb)
        for g in range(N_GROUPS):
            s_kv[g].append(_kv_rows(ka[:, None, :], va[:, None, :], g))
        s_st.append(st_s)

    fg = final_norm_g[None]
    y_prompt = _final_norm(xp, fg, tm_p).reshape(bsz, t, D_MODEL)
    y_sample = _final_norm(xs, fg, nb).reshape(nb, 1, D_MODEL)
    return (y_prompt, y_sample,
            jnp.stack(p_kv[0]), jnp.stack(p_kv[1]), jnp.stack(p_kv[2]), jnp.stack(p_st),
            jnp.stack(s_kv[0]), jnp.stack(s_kv[1]), jnp.stack(s_kv[2]), jnp.stack(s_st))
```

```python
import functools

import jax
import jax.numpy as jnp
from jax import lax
from jax.experimental import pallas as pl
from jax.experimental.pallas import tpu as pltpu

F32 = jnp.float32
BF16 = jnp.bfloat16

D_MODEL = 1024
PAST_LEN = 2048
A_GROUPS = ((128, 1), (512, 4), (2048, 16))
N_GROUPS = 3
A_HEADS = 4
A_HEAD_DIM = 64
A_ROT_DIM = A_HEAD_DIM // 4
ROPE_THETA = 500000.0
A_QKV = N_GROUPS * A_HEADS * A_HEAD_DIM
A_OUT = A_HEADS * A_HEAD_DIM
A_SPAN = 128
A_TOKENS = A_GROUPS[-1][1] * A_SPAN

B_HEADS = 4
B_DK = 128
B_DV = 256
B_KW = B_HEADS * B_DK
B_VW = B_HEADS * B_DV
B_GATE_RANK = 16
B_GATE_TAU = 16.0
D_FF = 4 * D_MODEL
EPS = 1e-6

LANES = 128
GLA_CHUNK = 128
GLA_SUB = 16
NEG_BIG = -1e30
VMEM_LIMIT = 56 * 1024 * 1024

GL_PAD = LANES
_SEG_SIZES = (A_QKV, A_QKV, A_QKV, B_KW, B_KW, B_VW, GL_PAD, B_VW, D_MODEL, D_MODEL)
_SEG_NAMES = ("qa", "ka", "va", "qb", "kb", "vb", "gl", "rb", "ga", "gb")
_SEG = {}
_off = 0
for _n, _s in zip(_SEG_NAMES, _SEG_SIZES):
    _SEG[_n] = (_off, _off + _s)
    _off += _s
IN_WIDTH_PAD = _off

NT_DIMS = (((1,), (1,)), ((), ()))
TN_DIMS = (((0,), (0,)), ((), ()))


def _params(sem):
    return pltpu.CompilerParams(dimension_semantics=sem, vmem_limit_bytes=VMEM_LIMIT)


def _sigmoid(x):
    return 1.0 / (1.0 + jnp.exp(-x))


def _log_sigmoid(x):
    return jnp.minimum(x, 0.0) - jnp.log(1.0 + jnp.exp(-jnp.abs(x)))


def _rms(x, g):
    ms = jnp.mean(x * x, axis=-1, keepdims=True)
    return x * lax.rsqrt(ms + EPS) * g


def _split_bf16(x):
    hi = x.astype(BF16)
    return hi, (x - hi.astype(F32)).astype(BF16)


def _inproj_kernel(x_ref, g_ref, w_ref, cos_ref, sa_ref, sb_ref, wgh_ref, wgl_ref, bg_ref, *refs, dilate):
    n_attn = 3 * N_GROUPS if dilate else 3
    attn_refs = refs[:n_attn]
    qb_ref, kb_ref, vb_ref, la_ref, rb_ref, ga_ref, gb_ref = refs[n_attn:n_attn + 7]
    scratch = refs[n_attn + 7:]
    tm = x_ref.shape[0]
    h = _rms(x_ref[...], g_ref[...]).astype(BF16)

    def seg(name):
        c0, c1 = _SEG[name]
        return jnp.dot(h, w_ref[:, c0:c1], preferred_element_type=F32)

    cosl, sa, sb = cos_ref[...], sa_ref[...], sb_ref[...]

    def rope(xc):
        return xc * cosl + pltpu.roll(xc, LANES - A_ROT_DIM // 2, 1) * sa + pltpu.roll(xc, A_ROT_DIM // 2, 1) * sb

    def attn_store(name, which, scale, rotate):
        acc = seg(name)
        tiles_per_group = A_OUT // LANES
        for g, (_, dil) in enumerate(A_GROUPS):
            for c in range(tiles_per_group):
                cg = g * tiles_per_group + c
                xc = acc[:, cg * LANES:(cg + 1) * LANES]
                if rotate:
                    xc = rope(xc)
                if scale != 1.0:
                    xc = xc * scale
                if not dilate:
                    attn_refs[which][:, cg * LANES:(cg + 1) * LANES] = xc.astype(attn_refs[which].dtype)
                    continue
                out_ref = attn_refs[which * N_GROUPS + g]
                ls = slice(c * LANES, (c + 1) * LANES)
                if dil == 1:
                    out_ref[0, 0, :, ls] = xc.astype(out_ref.dtype)
                else:
                    dscr = scratch[0]
                    dscr[...] = xc
                    for r in range(dil):
                        out_ref[0, r, :, ls] = dscr[pl.ds(r, tm // dil, stride=dil), :].astype(out_ref.dtype)

    attn_store("qa", 0, A_HEAD_DIM ** -0.5, True)
    attn_store("ka", 1, 1.0, True)
    attn_store("va", 2, 1.0, False)
    qb_ref[...] = (seg("qb") * (B_DK ** -0.5)).astype(qb_ref.dtype)
    kb_ref[...] = seg("kb").astype(kb_ref.dtype)
    vb_ref[...] = seg("vb").astype(vb_ref.dtype)
    gh, gl = _split_bf16(seg("gl"))
    wgh, wgl = wgh_ref[...], wgl_ref[...]
    xg = (jnp.dot(gh, wgh, preferred_element_type=F32) + jnp.dot(gl, wgh, preferred_element_type=F32)
          + jnp.dot(gh, wgl, preferred_element_type=F32) + bg_ref[...])
    la_ref[...] = _log_sigmoid(xg) * (1.0 / B_GATE_TAU)
    r = seg("rb")
    rb_ref[...] = (r * _sigmoid(r)).astype(rb_ref.dtype)
    ga_ref[...] = _sigmoid(seg("ga")).astype(ga_ref.dtype)
    gb_ref[...] = _sigmoid(seg("gb")).astype(gb_ref.dtype)


def _inproj(x, g1, w, tabs, wgh, wgl, bg, tm, seq_len=None):
    n = x.shape[0]
    dilate = seq_len is not None
    row = lambda i: (i, 0)
    const = lambda i: (0, 0)
    if dilate:
        tpb = seq_len // tm
        tab = lambda i: (i % tpb, 0)
        attn_shapes, attn_specs = [], []
        for _ in range(3):
            for _, dil in A_GROUPS:
                attn_shapes.append(jax.ShapeDtypeStruct((n // seq_len, dil, seq_len // dil, A_OUT), BF16))
                attn_specs.append(pl.BlockSpec((1, dil, tm // dil, A_OUT), lambda i: (i // tpb, 0, i % tpb, 0)))
        scratch = [pltpu.VMEM((tm, LANES), F32)]
    else:
        tab = row
        attn_shapes = [jax.ShapeDtypeStruct((n, A_QKV), F32)] * 3
        attn_specs = [pl.BlockSpec((tm, A_QKV), row)] * 3
        scratch = []
    rest = (("qb", B_KW, BF16), ("kb", B_KW, BF16), ("vb", B_VW, BF16), ("la", B_KW, F32),
            ("rb", B_VW, BF16), ("ga", D_MODEL, BF16), ("gb", D_MODEL, BF16))
    out_shape = attn_shapes + [jax.ShapeDtypeStruct((n, wd), dt) for _, wd, dt in rest]
    out_specs = attn_specs + [pl.BlockSpec((tm, wd), row) for _, wd, _ in rest]
    return pl.pallas_call(
        functools.partial(_inproj_kernel, dilate=dilate),
        grid=(n // tm,),
        in_specs=[
            pl.BlockSpec((tm, D_MODEL), row),
            pl.BlockSpec((1, D_MODEL), const),
            pl.BlockSpec((D_MODEL, IN_WIDTH_PAD), const, pipeline_mode=pl.Buffered(1)),
            pl.BlockSpec((tm, LANES), tab),
            pl.BlockSpec((tm, LANES), tab),
            pl.BlockSpec((tm, LANES), tab),
            pl.BlockSpec((GL_PAD, B_KW), const),
            pl.BlockSpec((GL_PAD, B_KW), const),
            pl.BlockSpec((1, B_KW), const),
        ],
        out_specs=out_specs,
        out_shape=out_shape,
        scratch_shapes=scratch,
        compiler_params=_params(("parallel",)),
        name="inproj_prompt" if dilate else "inproj_sample",
    )(x, g1, w, *tabs, wgh, wgl, bg)


def _attn_pair(qp, kw, vw, mask2, low):
    zero = jnp.zeros_like(qp)
    q2 = jnp.concatenate([jnp.where(low, qp, zero), jnp.where(low, zero, qp)], axis=0)
    s = lax.dot_general(q2, kw, NT_DIMS, preferred_element_type=F32)
    s = jnp.where(mask2, s, NEG_BIG)
    m = jnp.max(s, axis=1, keepdims=True)
    p = jnp.exp(s - m)
    l = jnp.sum(p, axis=1, keepdims=True)
    pv = jnp.dot(p.astype(BF16), vw, preferred_element_type=F32)
    o2 = pv / l
    lse2 = m + jnp.log(l)
    o = jnp.where(low, o2[0:A_SPAN], o2[A_SPAN:])
    lse = jnp.where(low, jnp.broadcast_to(lse2[0:A_SPAN], (A_SPAN, LANES)),
                    jnp.broadcast_to(lse2[A_SPAN:], (A_SPAN, LANES)))
    return o, lse


def _attn_kernel(*refs):
    ins = refs[:5 * N_GROUPS]
    o_ref = refs[5 * N_GROUPS]
    mx, sx, nx, ro, rl = refs[5 * N_GROUPS + 1:]
    first = pl.program_id(1) == 0
    row = lax.broadcasted_iota(jnp.int32, (2 * A_SPAN, 2 * A_SPAN), 0) & (A_SPAN - 1)
    col = lax.broadcasted_iota(jnp.int32, (2 * A_SPAN, 2 * A_SPAN), 1)
    band = (col >= row) & (col <= row + A_SPAN)
    band_first = band & (col >= jnp.where(first, A_SPAN, 0))
    low = lax.broadcasted_iota(jnp.int32, (A_SPAN, LANES), 1) < A_HEAD_DIM
    pairs = A_HEADS // 2

    for g, (_, dil) in enumerate(A_GROUPS):
        q_ref, kp_ref, kc_ref, vp_ref, vc_ref = ins[5 * g:5 * g + 5]
        nrow = A_TOKENS // dil
        nj = nrow // A_SPAN

        def emit(r, j, o, lse, hp):
            r0 = pl.multiple_of(j * A_SPAN, A_SPAN)
            if dil == 1:
                mx[hp, pl.ds(r0, A_SPAN), :] = lse
                sx[hp, pl.ds(r0, A_SPAN), :] = jnp.ones_like(lse)
                nx[hp, pl.ds(r0, A_SPAN), :] = o
            else:
                base = pl.multiple_of(r * nrow + r0, A_SPAN)
                ro[hp, pl.ds(base, A_SPAN), :] = o
                rl[hp, pl.ds(base, A_SPAN), :] = lse

        def first_block(r, carry):
            for hp in range(pairs):
                cs = slice(hp * LANES, (hp + 1) * LANES)
                kw = jnp.concatenate([kp_ref[0, r, :, cs], kc_ref[0, r, 0:A_SPAN, cs]], axis=0)
                vw = jnp.concatenate([vp_ref[0, r, :, cs], vc_ref[0, r, 0:A_SPAN, cs]], axis=0)
                o, lse = _attn_pair(q_ref[0, r, 0:A_SPAN, cs], kw, vw, band_first, low)
                emit(r, 0, o, lse, hp)
            return carry

        lax.fori_loop(0, dil, first_block, 0)

        if nj > 1:
            def later_blocks(r, carry):
                def later_block(j, inner):
                    k0 = pl.multiple_of((j - 1) * A_SPAN, A_SPAN)
                    q0 = pl.multiple_of(j * A_SPAN, A_SPAN)
                    for hp in range(pairs):
                        cs = slice(hp * LANES, (hp + 1) * LANES)
                        o, lse = _attn_pair(q_ref[0, r, pl.ds(q0, A_SPAN), cs],
                                            kc_ref[0, r, pl.ds(k0, 2 * A_SPAN), cs],
                                            vc_ref[0, r, pl.ds(k0, 2 * A_SPAN), cs], band, low)
                        emit(r, j, o, lse, hp)
                    return inner

                return lax.fori_loop(1, nj, later_block, carry)

            lax.fori_loop(0, dil, later_blocks, 0)

        if dil > 1:
            for hp in range(pairs):
                for r in range(dil):
                    rows = pl.ds(r, nrow, stride=dil)
                    lse = rl[hp, r * nrow:(r + 1) * nrow, :]
                    m_old = mx[hp, rows, :]
                    m_new = jnp.maximum(m_old, lse)
                    a_old = jnp.exp(m_old - m_new)
                    a_new = jnp.exp(lse - m_new)
                    mx[hp, rows, :] = m_new
                    sx[hp, rows, :] = sx[hp, rows, :] * a_old + a_new
                    nx[hp, rows, :] = nx[hp, rows, :] * a_old + a_new * ro[hp, r * nrow:(r + 1) * nrow, :]

    for hp in range(pairs):
        o_ref[0, :, hp * LANES:(hp + 1) * LANES] = (nx[hp] / sx[hp]).astype(o_ref.dtype)


def _attn_prompt(qkv, bsz, t):
    in_specs, args = [], []
    for g, (_, dil) in enumerate(A_GROUPS):
        nrow = A_TOKENS // dil
        sub = nrow // A_SPAN
        cur = pl.BlockSpec((1, dil, nrow, A_OUT), lambda b, i: (b, 0, i, 0))
        prev = pl.BlockSpec((1, dil, A_SPAN, A_OUT), lambda b, i, sub=sub: (b, 0, jnp.maximum(i * sub - 1, 0), 0))
        q, k, v = qkv[g], qkv[N_GROUPS + g], qkv[2 * N_GROUPS + g]
        in_specs += [cur, prev, cur, prev, cur]
        args += [q, k, k, v, v]
    o = pl.pallas_call(
        _attn_kernel,
        grid=(bsz, t // A_TOKENS),
        in_specs=in_specs,
        out_specs=pl.BlockSpec((1, A_TOKENS, A_OUT), lambda b, i: (b, i, 0)),
        out_shape=jax.ShapeDtypeStruct((bsz, t, A_OUT), BF16),
        scratch_shapes=[pltpu.VMEM((A_OUT // LANES, A_TOKENS, LANES), F32)] * 5,
        compiler_params=_params(("parallel", "parallel")),
        name="attn_prompt",
    )(*args)
    return o.reshape(bsz * t, A_OUT)


def _gla_kernel(q_ref, k_ref, v_ref, la_ref, r_ref, gn_ref, o_ref, sfin_ref, st_ref):
    c = pl.program_id(1)
    C = GLA_CHUNK
    nsub = C // GLA_SUB

    @pl.when(c == 0)
    def _():
        st_ref[...] = jnp.zeros_like(st_ref)

    ri = lax.broadcasted_iota(jnp.int32, (C, C), 0)
    ci = lax.broadcasted_iota(jnp.int32, (C, C), 1)
    tril = ci <= ri
    tril_b = jnp.where(tril, 1.0, 0.0).astype(BF16)
    la_hi, la_lo = _split_bf16(la_ref[0])
    b_all = (jnp.dot(tril_b, la_hi, preferred_element_type=F32)
             + jnp.dot(tril_b, la_lo, preferred_element_type=F32))
    srow = lax.broadcasted_iota(jnp.int32, (C, B_DK), 0)
    gn = gn_ref[...]

    for h in range(B_HEADS):
        ks = slice(h * B_DK, (h + 1) * B_DK)
        vs = slice(h * B_DV, (h + 1) * B_DV)
        b = b_all[:, ks]
        q = q_ref[0, :, ks].astype(F32)
        k = k_ref[0, :, ks].astype(F32)
        v = v_ref[0, :, vs]
        st = st_ref[h]
        o_inter = lax.dot_general((q * jnp.exp(b)).astype(BF16), st.astype(BF16), NT_DIMS,
                                  preferred_element_type=F32)
        beta = b.reshape(nsub, GLA_SUB, B_DK)[:, 0:1, :]
        beta_rows = jnp.broadcast_to(beta, (nsub, GLA_SUB, B_DK)).reshape(C, B_DK)
        qt = (q * jnp.exp(b - beta_rows)).astype(BF16)
        kts = []
        for i in range(nsub):
            bi = b[GLA_SUB * i:GLA_SUB * i + 1, :]
            e = jnp.exp(jnp.where(srow < GLA_SUB * (i + 1), bi - b, NEG_BIG))
            kts.append((k * e).astype(BF16))
        att_all = lax.dot_general(qt, jnp.concatenate(kts, axis=0), NT_DIMS, preferred_element_type=F32)
        att = jnp.concatenate(
            [att_all[GLA_SUB * i:GLA_SUB * (i + 1), i * C:(i + 1) * C] for i in range(nsub)], axis=0)
        att = jnp.where(tril, att, 0.0).astype(BF16)
        o = o_inter + jnp.dot(att, v, preferred_element_type=F32)
        bl = b[C - 1:C, :]
        kh = (k * jnp.exp(bl - b)).astype(BF16)
        st_ref[h] = st * jnp.exp(bl) + lax.dot_general(v, kh, TN_DIMS, preferred_element_type=F32)
        y = _rms(o, gn) * r_ref[0, :, vs].astype(F32)
        o_ref[0, :, vs] = y.astype(o_ref.dtype)

    @pl.when(c == pl.num_programs(1) - 1)
    def _():
        for h in range(B_HEADS):
            sfin_ref[0, h] = st_ref[h].T


def _gla_prompt(qb, kb, vb, la, rb, gn, bsz, t):
    C = GLA_CHUNK
    blk = lambda b, c: (b, c, 0)
    o, sfin = pl.pallas_call(
        _gla_kernel,
        grid=(bsz, t // C),
        in_specs=[
            pl.BlockSpec((1, C, B_KW), blk),
            pl.BlockSpec((1, C, B_KW), blk),
            pl.BlockSpec((1, C, B_VW), blk),
            pl.BlockSpec((1, C, B_KW), blk),
            pl.BlockSpec((1, C, B_VW), blk),
            pl.BlockSpec((1, B_DV), lambda b, c: (0, 0)),
        ],
        out_specs=[pl.BlockSpec((1, C, B_VW), blk),
                   pl.BlockSpec((1, B_HEADS, B_DK, B_DV), lambda b, c: (b, 0, 0, 0))],
        out_shape=[jax.ShapeDtypeStruct((bsz, t, B_VW), BF16),
                   jax.ShapeDtypeStruct((bsz, B_HEADS, B_DK, B_DV), F32)],
        scratch_shapes=[pltpu.VMEM((B_HEADS, B_DV, B_DK), F32)],
        compiler_params=_params(("parallel", "arbitrary")),
        name="gla_prompt",
    )(qb.reshape(bsz, t, B_KW), kb.reshape(bsz, t, B_KW), vb.reshape(bsz, t, B_VW),
      la.reshape(bsz, t, B_KW), rb.reshape(bsz, t, B_VW), gn)
    return o.reshape(bsz * t, B_VW), sfin


def _merge_kernel(oa_ref, ob_ref, ga_ref, gb_ref, x_ref, wpa_ref, wpb_ref, wo_ref, g2_ref, xo_ref, h2_ref):
    ya = jnp.dot(oa_ref[...].astype(BF16), wpa_ref[...], preferred_element_type=F32)
    yb = jnp.dot(ob_ref[...].astype(BF16), wpb_ref[...], preferred_element_type=F32)
    mixed = ga_ref[...].astype(F32) * ya + gb_ref[...].astype(F32) * yb
    xn = x_ref[...] + jnp.dot(mixed.astype(BF16), wo_ref[...], preferred_element_type=F32)
    xo_ref[...] = xn
    h2_ref[...] = _rms(xn, g2_ref[...]).astype(h2_ref.dtype)


def _merge(oa, ob, ga, gb, x, wpa, wpb, wo, g2, tm):
    n = x.shape[0]
    row = lambda i: (i, 0)
    const = lambda i: (0, 0)
    return pl.pallas_call(
        _merge_kernel,
        grid=(n // tm,),
        in_specs=[
            pl.BlockSpec((tm, A_OUT), row),
            pl.BlockSpec((tm, B_VW), row),
            pl.BlockSpec((tm, D_MODEL), row),
            pl.BlockSpec((tm, D_MODEL), row),
            pl.BlockSpec((tm, D_MODEL), row),
            pl.BlockSpec((A_OUT, D_MODEL), const),
            pl.BlockSpec((B_VW, D_MODEL), const),
            pl.BlockSpec((D_MODEL, D_MODEL), const),
            pl.BlockSpec((1, D_MODEL), const),
        ],
        out_specs=[pl.BlockSpec((tm, D_MODEL), row), pl.BlockSpec((tm, D_MODEL), row)],
        out_shape=[jax.ShapeDtypeStruct((n, D_MODEL), F32), jax.ShapeDtypeStruct((n, D_MODEL), BF16)],
        compiler_params=_params(("parallel",)),
        name="merge",
    )(oa, ob, ga, gb, x, wpa, wpb, wo, g2)


def _ffn_kernel(h_ref, x_ref, w1_ref, w2_ref, o_ref):
    @pl.when(pl.program_id(1) == 0)
    def _():
        o_ref[...] = x_ref[...]

    a = jnp.maximum(jnp.dot(h_ref[...], w1_ref[...], preferred_element_type=F32), 0.0)
    o_ref[...] += jnp.dot((a * a).astype(BF16), w2_ref[...], preferred_element_type=F32)


def _ffn(h2, x, w1, w2, tm, tf=1024):
    n = x.shape[0]
    return pl.pallas_call(
        _ffn_kernel,
        grid=(n // tm, D_FF // tf),
        in_specs=[
            pl.BlockSpec((tm, D_MODEL), lambda i, k: (i, 0)),
            pl.BlockSpec((tm, D_MODEL), lambda i, k: (i, 0)),
            pl.BlockSpec((D_MODEL, tf), lambda i, k: (0, k)),
            pl.BlockSpec((tf, D_MODEL), lambda i, k: (k, 0)),
        ],
        out_specs=pl.BlockSpec((tm, D_MODEL), lambda i, k: (i, 0)),
        out_shape=jax.ShapeDtypeStruct((n, D_MODEL), F32),
        compiler_params=_params(("parallel", "arbitrary")),
        name="ffn",
    )(h2, x, w1, w2)


def _final_norm_kernel(x_ref, g_ref, o_ref):
    o_ref[...] = _rms(x_ref[...], g_ref[...])


def _final_norm(x, g, tm):
    n = x.shape[0]
    return pl.pallas_call(
        _final_norm_kernel,
        grid=(n // tm,),
        in_specs=[pl.BlockSpec((tm, D_MODEL), lambda i: (i, 0)), pl.BlockSpec((1, D_MODEL), lambda i: (0, 0))],
        out_specs=pl.BlockSpec((tm, D_MODEL), lambda i: (i, 0)),
        out_shape=jax.ShapeDtypeStruct((n, D_MODEL), F32),
        compiler_params=_params(("parallel",)),
        name="final_norm",
    )(x, g)


def _sattn_kernel(q_ref, k_ref, v_ref, c1_ref, c2_ref, c3_ref, o_ref, *, bt):
    for b in range(bt):
        outs, lses = [], []
        for g, c_ref in enumerate((c1_ref, c2_ref, c3_ref)):
            q = q_ref[b, g]
            kn = k_ref[b, g]
            vn = v_ref[b, g]
            kc = c_ref[0, b, :, 0, 0]
            vc = c_ref[0, b, :, 0, 1]
            s = jnp.sum(kc * q[None], axis=-1, keepdims=True)
            sn = jnp.sum(kn * q, axis=-1, keepdims=True)
            m = jnp.maximum(jnp.max(s, axis=0), sn)
            p = jnp.exp(s - m[None])
            pn = jnp.exp(sn - m)
            l = jnp.sum(p, axis=0) + pn
            outs.append((jnp.sum(p * vc, axis=0) + pn * vn) / l)
            lses.append(m + jnp.log(l))
        m = jnp.maximum(jnp.maximum(lses[0], lses[1]), lses[2])
        es = [jnp.exp(x - m) for x in lses]
        o_ref[b] = (es[0] * outs[0] + es[1] * outs[1] + es[2] * outs[2]) / (es[0] + es[1] + es[2])


def _sattn(qa, ka, va, caches, layer, bt=4):
    nb = qa.shape[0]
    hshape = (nb, N_GROUPS, A_HEADS, A_HEAD_DIM)
    views, cspecs = [], []
    for c, (window, dil) in zip(caches, A_GROUPS):
        depth, _, buf_len = c.shape[:3]
        views.append(c.reshape(depth, nb, buf_len // dil, dil, 2, A_HEADS, A_HEAD_DIM))
        cspecs.append(pl.BlockSpec((1, bt, A_SPAN, 1, 2, A_HEADS, A_HEAD_DIM),
                                   lambda i: (layer, i, 0, 0, 0, 0, 0)))
    hspec = pl.BlockSpec((bt,) + hshape[1:], lambda i: (i, 0, 0, 0))
    o = pl.pallas_call(
        functools.partial(_sattn_kernel, bt=bt),
        grid=(nb // bt,),
        in_specs=[hspec] * 3 + cspecs,
        out_specs=pl.BlockSpec((bt, A_HEADS, A_HEAD_DIM), lambda i: (i, 0, 0)),
        out_shape=jax.ShapeDtypeStruct((nb, A_HEADS, A_HEAD_DIM), F32),
        compiler_params=_params(("parallel",)),
        name="attn_sample",
    )(qa.reshape(hshape), ka.reshape(hshape), va.reshape(hshape), *views)
    return o.reshape(nb, A_OUT)


def _sgla_kernel(q_ref, k_ref, v_ref, la_ref, r_ref, gn_ref, s_ref, o_ref, so_ref, oraw, *, bt):
    a = jnp.exp(la_ref[...])
    q = q_ref[...].astype(F32)
    k = k_ref[...].astype(F32)
    v = v_ref[...].astype(F32)

    def column(arr, b, h):
        r = arr[b:b + 1, h * B_DK:(h + 1) * B_DK]
        return jnp.broadcast_to(r, (LANES, B_DK)).T

    for b in range(bt):
        for h in range(B_HEADS):
            ac, kc, qc = column(a, b, h), column(k, b, h), column(q, b, h)
            for half in range(B_DV // LANES):
                ls = slice(half * LANES, (half + 1) * LANES)
                vh = v[b:b + 1, h * B_DV + half * LANES:h * B_DV + (half + 1) * LANES]
                sn = ac * s_ref[0, b, h, :, ls] + kc * vh
                so_ref[b, h, :, ls] = sn
                oraw[b:b + 1, h * B_DV + half * LANES:h * B_DV + (half + 1) * LANES] = jnp.sum(
                    qc * sn, axis=0, keepdims=True)
    gn = gn_ref[...]
    for h in range(B_HEADS):
        vs = slice(h * B_DV, (h + 1) * B_DV)
        o_ref[:, vs] = _rms(oraw[:, vs], gn) * r_ref[:, vs].astype(F32)


def _sgla(qb, kb, vb, la, rb, gn, state, layer, bt=8):
    nb = qb.shape[0]
    row = lambda i: (i, 0)
    sblk = (1, bt, B_HEADS, B_DK, B_DV)
    return pl.pallas_call(
        functools.partial(_sgla_kernel, bt=bt),
        grid=(nb // bt,),
        in_specs=[
            pl.BlockSpec((bt, B_KW), row),
            pl.BlockSpec((bt, B_KW), row),
            pl.BlockSpec((bt, B_VW), row),
            pl.BlockSpec((bt, B_KW), row),
            pl.BlockSpec((bt, B_VW), row),
            pl.BlockSpec((1, B_DV), lambda i: (0, 0)),
            pl.BlockSpec(sblk, lambda i: (layer, i, 0, 0, 0)),
        ],
        out_specs=[pl.BlockSpec((bt, B_VW), row), pl.BlockSpec(sblk[1:], lambda i: (i, 0, 0, 0))],
        out_shape=[jax.ShapeDtypeStruct((nb, B_VW), F32),
                   jax.ShapeDtypeStruct((nb, B_HEADS, B_DK, B_DV), F32)],
        scratch_shapes=[pltpu.VMEM((bt, B_VW), F32)],
        compiler_params=_params(("parallel",)),
        name="gla_sample",
    )(qb, kb, vb, la, rb, gn, state)


def _rope_tables(pos):
    half = A_ROT_DIM // 2
    inv_freq = 1.0 / (ROPE_THETA ** (jnp.arange(half, dtype=F32) / half))
    ang = pos.astype(F32)[:, None] * inv_freq[None, :]
    cos, sin = jnp.cos(ang), jnp.sin(ang)
    n = pos.shape[0]
    pad = A_HEAD_DIM - A_ROT_DIM
    cos_h = jnp.concatenate([cos, cos, jnp.ones((n, pad), F32)], axis=1)
    sa_h = jnp.concatenate([-sin, jnp.zeros((n, half + pad), F32)], axis=1)
    sb_h = jnp.concatenate([jnp.zeros((n, half), F32), sin, jnp.zeros((n, pad), F32)], axis=1)
    rep = LANES // A_HEAD_DIM
    return jnp.tile(cos_h, (1, rep)), jnp.tile(sa_h, (1, rep)), jnp.tile(sb_h, (1, rep))


def _kv_rows(k, v):
    k = k.astype(F32).reshape(k.shape[:-1] + (A_HEADS, A_HEAD_DIM))
    v = v.astype(F32).reshape(v.shape[:-1] + (A_HEADS, A_HEAD_DIM))
    return jnp.stack([k, v], axis=-3)


def _last_tokens(x, keep):
    bsz, dil, n, w = x.shape
    rows = keep // dil
    return jnp.swapaxes(x[:, :, n - rows:, :], 1, 2).reshape(bsz, keep, w)


def kernel(x_prompt, x_sample, cache_a1_kv, cache_a2_kv, cache_a3_kv, state_gla, norm1_g, w_in, w_gate_up, b_gate,
           gla_norm_g, w_pa, w_pb, w_o, norm2_g, w_ff1, w_ff2, final_norm_g):
    bsz, t, _ = x_prompt.shape
    nb = x_sample.shape[0]
    depth = w_in.shape[0]
    assert x_sample.shape[1] == 1 and t % A_TOKENS == 0 and t % GLA_CHUNK == 0
    caches = (cache_a1_kv, cache_a2_kv, cache_a3_kv)
    for c, (window, _) in zip(caches, A_GROUPS):
        assert c.shape[2] == window

    gl0, gl1 = _SEG["gl"][0], _SEG["gl"][0] + B_GATE_RANK
    w_in_p = jnp.concatenate(
        [w_in[:, :, :gl0], w_in[:, :, gl0:gl1], jnp.zeros((depth, D_MODEL, GL_PAD - B_GATE_RANK), w_in.dtype),
         w_in[:, :, gl1:]], axis=2).astype(BF16)
    wgu_p = jnp.concatenate([w_gate_up, jnp.zeros((depth, GL_PAD - B_GATE_RANK, B_KW), F32)], axis=1)
    wgu_hi = wgu_p.astype(BF16)
    wgu_lo = (wgu_p - wgu_hi.astype(F32)).astype(BF16)
    w_pa_b, w_pb_b, w_o_b = w_pa.astype(BF16), w_pb.astype(BF16), w_o.astype(BF16)
    w_ff1_b, w_ff2_b = w_ff1.astype(BF16), w_ff2.astype(BF16)

    tm_p = 512
    tabs_p = _rope_tables(jnp.arange(t, dtype=jnp.int32))
    tabs_s = _rope_tables(jnp.full((nb,), PAST_LEN, dtype=jnp.int32))

    xp = x_prompt.reshape(bsz * t, D_MODEL)
    xs = x_sample.reshape(nb, D_MODEL)
    p_kv = [[], [], []]
    s_kv = [[], [], []]
    p_st, s_st = [], []
    for l in range(depth):
        g1 = norm1_g[l][None]
        g2 = norm2_g[l][None]
        bg = b_gate[l][None]
        gn = gla_norm_g[l][None]

        outs = _inproj(xp, g1, w_in_p[l], tabs_p, wgu_hi[l], wgu_lo[l], bg, tm_p, seq_len=t)
        qkv, (qb, kb, vb, la, rb, ga, gb) = outs[:3 * N_GROUPS], outs[3 * N_GROUPS:]
        oa = _attn_prompt(qkv, bsz, t)
        ob, st_p = _gla_prompt(qb, kb, vb, la, rb, gn, bsz, t)
        xp, h2 = _merge(oa, ob, ga, gb, xp, w_pa_b[l], w_pb_b[l], w_o_b[l], g2, tm_p)
        xp = _ffn(h2, xp, w_ff1_b[l], w_ff2_b[l], 1024)
        for g, (window, _) in enumerate(A_GROUPS):
            keep = min(window, t)
            p_kv[g].append(_kv_rows(_last_tokens(qkv[N_GROUPS + g], keep), _last_tokens(qkv[2 * N_GROUPS + g], keep)))
        p_st.append(st_p)

        qa, ka, va, qb, kb, vb, la, rb, ga, gb = _inproj(xs, g1, w_in_p[l], tabs_s, wgu_hi[l], wgu_lo[l], bg, nb)
        oa = _sattn(qa, ka, va, caches, l)
        ob, st_s = _sgla(qb, kb, vb, la, rb, gn, state_gla, l)
        xs, h2 = _merge(oa, ob, ga, gb, xs, w_pa_b[l], w_pb_b[l], w_o_b[l], g2, nb)
        xs = _ffn(h2, xs, w_ff1_b[l], w_ff2_b[l], nb)
        for g in range(N_GROUPS):
            gs = slice(g * A_OUT, (g + 1) * A_OUT)
            s_kv[g].append(_kv_rows(ka[:, None, gs], va[:, None, gs]))
        s_st.append(st_s)

    fg = final_norm_g[None]
    y_prompt = _final_norm(xp, fg, tm_p).reshape(bsz, t, D_MODEL)
    y_sample = _final_norm(xs, fg, nb).reshape(nb, 1, D_MODEL)
    return (y_prompt, y_sample,
            jnp.stack(p_kv[0]), jnp.stack(p_kv[1]), jnp.stack(p_kv[2]), jnp.stack(p_st),
            jnp.stack(s_kv[0]), jnp.stack(s_kv[1]), jnp.stack(s_kv[2]), jnp.stack(s_st))
```

```python
import functools

import jax
import jax.numpy as jnp
from jax import lax
from jax.experimental import pallas as pl
from jax.experimental.pallas import tpu as pltpu

F32 = jnp.float32
BF16 = jnp.bfloat16

D_MODEL = 1024
PAST_LEN = 2048
A_GROUPS = ((128, 1), (512, 4), (2048, 16))
N_GROUPS = 3
A_HEADS = 4
A_HEAD_DIM = 64
A_ROT_DIM = A_HEAD_DIM // 4
ROPE_THETA = 500000.0
A_QKV = N_GROUPS * A_HEADS * A_HEAD_DIM
A_OUT = A_HEADS * A_HEAD_DIM
A_SPAN = 128
A_TOKENS = A_GROUPS[-1][1] * A_SPAN

B_HEADS = 4
B_DK = 128
B_DV = 256
B_KW = B_HEADS * B_DK
B_VW = B_HEADS * B_DV
B_GATE_RANK = 16
B_GATE_TAU = 16.0
D_FF = 4 * D_MODEL
EPS = 1e-6

LANES = 128
GLA_CHUNK = 128
GLA_SUB = 16
ATTN_UNROLL_FIRST = 4
ATTN_UNROLL_LATER = 5
NEG_BIG = -1e30
VMEM_LIMIT = 56 * 1024 * 1024

GL_PAD = LANES
_SEG_SIZES = (A_QKV, A_QKV, A_QKV, B_KW, B_KW, B_VW, GL_PAD, B_VW, D_MODEL, D_MODEL)
_SEG_NAMES = ("qa", "ka", "va", "qb", "kb", "vb", "gl", "rb", "ga", "gb")
_SEG = {}
_off = 0
for _n, _s in zip(_SEG_NAMES, _SEG_SIZES):
    _SEG[_n] = (_off, _off + _s)
    _off += _s
IN_WIDTH_PAD = _off

NT_DIMS = (((1,), (1,)), ((), ()))
TN_DIMS = (((0,), (0,)), ((), ()))


def _params(sem):
    return pltpu.CompilerParams(dimension_semantics=sem, vmem_limit_bytes=VMEM_LIMIT)


def _sigmoid(x):
    return 1.0 / (1.0 + jnp.exp(-x))


def _log_sigmoid(x):
    return jnp.minimum(x, 0.0) - jnp.log(1.0 + jnp.exp(-jnp.abs(x)))


def _rms(x, g):
    ms = jnp.mean(x * x, axis=-1, keepdims=True)
    return x * lax.rsqrt(ms + EPS) * g


def _split_bf16(x):
    hi = x.astype(BF16)
    return hi, (x - hi.astype(F32)).astype(BF16)


def _inproj_kernel(x_ref, g_ref, w_ref, cos_ref, sa_ref, sb_ref, wgh_ref, wgl_ref, bg_ref, *refs, dilate):
    n_attn = 3 * N_GROUPS if dilate else 3
    attn_refs = refs[:n_attn]
    qb_ref, kb_ref, vb_ref, la_ref, rb_ref, ga_ref, gb_ref = refs[n_attn:n_attn + 7]
    scratch = refs[n_attn + 7:]
    tm = x_ref.shape[0]
    h = _rms(x_ref[...], g_ref[...]).astype(BF16)

    def seg(name):
        c0, c1 = _SEG[name]
        return jnp.dot(h, w_ref[:, c0:c1], preferred_element_type=F32)

    cosl, sa, sb = cos_ref[...], sa_ref[...], sb_ref[...]

    def rope(xc):
        return xc * cosl + pltpu.roll(xc, LANES - A_ROT_DIM // 2, 1) * sa + pltpu.roll(xc, A_ROT_DIM // 2, 1) * sb

    def attn_store(name, which, scale, rotate):
        acc = seg(name)
        tiles_per_group = A_OUT // LANES
        for g, (_, dil) in enumerate(A_GROUPS):
            for c in range(tiles_per_group):
                cg = g * tiles_per_group + c
                xc = acc[:, cg * LANES:(cg + 1) * LANES]
                if rotate:
                    xc = rope(xc)
                if scale != 1.0:
                    xc = xc * scale
                if not dilate:
                    attn_refs[which][:, cg * LANES:(cg + 1) * LANES] = xc.astype(attn_refs[which].dtype)
                    continue
                out_ref = attn_refs[which * N_GROUPS + g]
                ls = slice(c * LANES, (c + 1) * LANES)
                if dil == 1:
                    out_ref[0, 0, :, ls] = xc.astype(out_ref.dtype)
                else:
                    dscr = scratch[0]
                    dscr[...] = xc
                    for r in range(dil):
                        out_ref[0, r, :, ls] = dscr[pl.ds(r, tm // dil, stride=dil), :].astype(out_ref.dtype)

    attn_store("qa", 0, A_HEAD_DIM ** -0.5, True)
    attn_store("ka", 1, 1.0, True)
    attn_store("va", 2, 1.0, False)
    qb_ref[...] = (seg("qb") * (B_DK ** -0.5)).astype(qb_ref.dtype)
    kb_ref[...] = seg("kb").astype(kb_ref.dtype)
    vb_ref[...] = seg("vb").astype(vb_ref.dtype)
    gh, gl = _split_bf16(seg("gl"))
    wgh, wgl = wgh_ref[...], wgl_ref[...]
    xg = (jnp.dot(gh, wgh, preferred_element_type=F32) + jnp.dot(gl, wgh, preferred_element_type=F32)
          + jnp.dot(gh, wgl, preferred_element_type=F32) + bg_ref[...])
    la_ref[...] = _log_sigmoid(xg) * (1.0 / B_GATE_TAU)
    r = seg("rb")
    rb_ref[...] = (r * _sigmoid(r)).astype(rb_ref.dtype)
    ga_ref[...] = _sigmoid(seg("ga")).astype(ga_ref.dtype)
    gb_ref[...] = _sigmoid(seg("gb")).astype(gb_ref.dtype)


def _inproj(x, g1, w, tabs, wgh, wgl, bg, tm, seq_len=None):
    n = x.shape[0]
    dilate = seq_len is not None
    row = lambda i: (i, 0)
    const = lambda i: (0, 0)
    if dilate:
        tpb = seq_len // tm
        tab = lambda i: (i % tpb, 0)
        attn_shapes, attn_specs = [], []
        for _ in range(3):
            for _, dil in A_GROUPS:
                attn_shapes.append(jax.ShapeDtypeStruct((n // seq_len, dil, seq_len // dil, A_OUT), BF16))
                attn_specs.append(pl.BlockSpec((1, dil, tm // dil, A_OUT), lambda i: (i // tpb, 0, i % tpb, 0)))
        scratch = [pltpu.VMEM((tm, LANES), F32)]
    else:
        tab = row
        attn_shapes = [jax.ShapeDtypeStruct((n, A_QKV), F32)] * 3
        attn_specs = [pl.BlockSpec((tm, A_QKV), row)] * 3
        scratch = []
    rest = (("qb", B_KW, BF16), ("kb", B_KW, BF16), ("vb", B_VW, BF16), ("la", B_KW, F32),
            ("rb", B_VW, BF16), ("ga", D_MODEL, BF16), ("gb", D_MODEL, BF16))
    out_shape = attn_shapes + [jax.ShapeDtypeStruct((n, wd), dt) for _, wd, dt in rest]
    out_specs = attn_specs + [pl.BlockSpec((tm, wd), row) for _, wd, _ in rest]
    return pl.pallas_call(
        functools.partial(_inproj_kernel, dilate=dilate),
        grid=(n // tm,),
        in_specs=[
            pl.BlockSpec((tm, D_MODEL), row),
            pl.BlockSpec((1, D_MODEL), const),
            pl.BlockSpec((D_MODEL, IN_WIDTH_PAD), const, pipeline_mode=pl.Buffered(1)),
            pl.BlockSpec((tm, LANES), tab),
            pl.BlockSpec((tm, LANES), tab),
            pl.BlockSpec((tm, LANES), tab),
            pl.BlockSpec((GL_PAD, B_KW), const),
            pl.BlockSpec((GL_PAD, B_KW), const),
            pl.BlockSpec((1, B_KW), const),
        ],
        out_specs=out_specs,
        out_shape=out_shape,
        scratch_shapes=scratch,
        compiler_params=_params(("parallel",)),
        name="inproj_prompt" if dilate else "inproj_sample",
    )(x, g1, w, *tabs, wgh, wgl, bg)


def _attn_pair(qp, kw, vw, mask2, low):
    zero = jnp.zeros_like(qp)
    q2 = jnp.concatenate([jnp.where(low, qp, zero), jnp.where(low, zero, qp)], axis=0)
    s = lax.dot_general(q2, kw, NT_DIMS, preferred_element_type=F32)
    s = jnp.where(mask2, s, NEG_BIG)
    m = jnp.max(s, axis=1, keepdims=True)
    p = jnp.exp(s - m)
    l = jnp.sum(p, axis=1, keepdims=True)
    pv = jnp.dot(p.astype(BF16), vw, preferred_element_type=F32)
    o2 = pv / l
    lse2 = m + jnp.log(l)
    o = jnp.where(low, o2[0:A_SPAN], o2[A_SPAN:])
    lse = jnp.where(low, jnp.broadcast_to(lse2[0:A_SPAN], (A_SPAN, LANES)),
                    jnp.broadcast_to(lse2[A_SPAN:], (A_SPAN, LANES)))
    return o, lse


def _attn_kernel(*refs):
    ins = refs[:5 * N_GROUPS]
    o_ref = refs[5 * N_GROUPS]
    mx, sx, nx, ro, rl = refs[5 * N_GROUPS + 1:]
    first = pl.program_id(1) == 0
    row = lax.broadcasted_iota(jnp.int32, (2 * A_SPAN, 2 * A_SPAN), 0) & (A_SPAN - 1)
    col = lax.broadcasted_iota(jnp.int32, (2 * A_SPAN, 2 * A_SPAN), 1)
    band = (col >= row) & (col <= row + A_SPAN)
    band_first = band & (col >= jnp.where(first, A_SPAN, 0))
    low = lax.broadcasted_iota(jnp.int32, (A_SPAN, LANES), 1) < A_HEAD_DIM
    pairs = A_HEADS // 2

    for g, (_, dil) in enumerate(A_GROUPS):
        q_ref, kp_ref, kc_ref, vp_ref, vc_ref = ins[5 * g:5 * g + 5]
        nrow = A_TOKENS // dil
        nj = nrow // A_SPAN

        def emit(r, j, o, lse, hp):
            r0 = pl.multiple_of(j * A_SPAN, A_SPAN)
            if dil == 1:
                mx[hp, pl.ds(r0, A_SPAN), :] = lse
                sx[hp, pl.ds(r0, A_SPAN), :] = jnp.ones_like(lse)
                nx[hp, pl.ds(r0, A_SPAN), :] = o
            else:
                base = pl.multiple_of(r * nrow + r0, A_SPAN)
                ro[hp, pl.ds(base, A_SPAN), :] = o
                rl[hp, pl.ds(base, A_SPAN), :] = lse

        def first_block(r, carry):
            for hp in range(pairs):
                cs = slice(hp * LANES, (hp + 1) * LANES)
                kw = jnp.concatenate([kp_ref[0, r, :, cs], kc_ref[0, r, 0:A_SPAN, cs]], axis=0)
                vw = jnp.concatenate([vp_ref[0, r, :, cs], vc_ref[0, r, 0:A_SPAN, cs]], axis=0)
                o, lse = _attn_pair(q_ref[0, r, 0:A_SPAN, cs], kw, vw, band_first, low)
                emit(r, 0, o, lse, hp)
            return carry

        lax.fori_loop(0, dil, first_block, 0, unroll=min(dil, ATTN_UNROLL_FIRST))

        if nj > 1:
            def later_blocks(r, carry):
                def later_block(j, inner):
                    k0 = pl.multiple_of((j - 1) * A_SPAN, A_SPAN)
                    q0 = pl.multiple_of(j * A_SPAN, A_SPAN)
                    for hp in range(pairs):
                        cs = slice(hp * LANES, (hp + 1) * LANES)
                        o, lse = _attn_pair(q_ref[0, r, pl.ds(q0, A_SPAN), cs],
                                            kc_ref[0, r, pl.ds(k0, 2 * A_SPAN), cs],
                                            vc_ref[0, r, pl.ds(k0, 2 * A_SPAN), cs], band, low)
                        emit(r, j, o, lse, hp)
                    return inner

                return lax.fori_loop(1, nj, later_block, carry, unroll=min(nj - 1, ATTN_UNROLL_LATER))

            lax.fori_loop(0, dil, later_blocks, 0)

        if dil > 1:
            for hp in range(pairs):
                for r in range(dil):
                    rows = pl.ds(r, nrow, stride=dil)
                    lse = rl[hp, r * nrow:(r + 1) * nrow, :]
                    m_old = mx[hp, rows, :]
                    m_new = jnp.maximum(m_old, lse)
                    a_old = jnp.exp(m_old - m_new)
                    a_new = jnp.exp(lse - m_new)
                    mx[hp, rows, :] = m_new
                    sx[hp, rows, :] = sx[hp, rows, :] * a_old + a_new
                    nx[hp, rows, :] = nx[hp, rows, :] * a_old + a_new * ro[hp, r * nrow:(r + 1) * nrow, :]

    for hp in range(pairs):
        o_ref[0, :, hp * LANES:(hp + 1) * LANES] = (nx[hp] / sx[hp]).astype(o_ref.dtype)


def _attn_prompt(qkv, bsz, t):
    in_specs, args = [], []
    for g, (_, dil) in enumerate(A_GROUPS):
        nrow = A_TOKENS // dil
        sub = nrow // A_SPAN
        cur = pl.BlockSpec((1, dil, nrow, A_OUT), lambda b, i: (b, 0, i, 0))
        prev = pl.BlockSpec((1, dil, A_SPAN, A_OUT), lambda b, i, sub=sub: (b, 0, jnp.maximum(i * sub - 1, 0), 0))
        q, k, v = qkv[g], qkv[N_GROUPS + g], qkv[2 * N_GROUPS + g]
        in_specs += [cur, prev, cur, prev, cur]
        args += [q, k, k, v, v]
    o = pl.pallas_call(
        _attn_kernel,
        grid=(bsz, t // A_TOKENS),
        in_specs=in_specs,
        out_specs=pl.BlockSpec((1, A_TOKENS, A_OUT), lambda b, i: (b, i, 0)),
        out_shape=jax.ShapeDtypeStruct((bsz, t, A_OUT), BF16),
        scratch_shapes=[pltpu.VMEM((A_OUT // LANES, A_TOKENS, LANES), F32)] * 5,
        compiler_params=_params(("parallel", "parallel")),
        name="attn_prompt",
    )(*args)
    return o.reshape(bsz * t, A_OUT)


def _gla_kernel(q_ref, k_ref, v_ref, la_ref, r_ref, gn_ref, o_ref, sfin_ref, st_ref):
    c = pl.program_id(1)
    C = GLA_CHUNK
    nsub = C // GLA_SUB

    @pl.when(c == 0)
    def _():
        st_ref[...] = jnp.zeros_like(st_ref)

    ri = lax.broadcasted_iota(jnp.int32, (C, C), 0)
    ci = lax.broadcasted_iota(jnp.int32, (C, C), 1)
    tril = ci <= ri
    tril_b = jnp.where(tril, 1.0, 0.0).astype(BF16)
    la_hi, la_lo = _split_bf16(la_ref[0])
    b_all = (jnp.dot(tril_b, la_hi, preferred_element_type=F32)
             + jnp.dot(tril_b, la_lo, preferred_element_type=F32))
    srow = lax.broadcasted_iota(jnp.int32, (C, B_DK), 0)
    gn = gn_ref[...]

    for h in range(B_HEADS):
        ks = slice(h * B_DK, (h + 1) * B_DK)
        vs = slice(h * B_DV, (h + 1) * B_DV)
        b = b_all[:, ks]
        q = q_ref[0, :, ks].astype(F32)
        k = k_ref[0, :, ks].astype(F32)
        v = v_ref[0, :, vs]
        st = st_ref[h]
        o_inter = lax.dot_general((q * jnp.exp(b)).astype(BF16), st.astype(BF16), NT_DIMS,
                                  preferred_element_type=F32)
        beta = b.reshape(nsub, GLA_SUB, B_DK)[:, 0:1, :]
        beta_rows = jnp.broadcast_to(beta, (nsub, GLA_SUB, B_DK)).reshape(C, B_DK)
        qt = (q * jnp.exp(b - beta_rows)).astype(BF16)
        kts = []
        for i in range(nsub):
            bi = b[GLA_SUB * i:GLA_SUB * i + 1, :]
            e = jnp.exp(jnp.where(srow < GLA_SUB * (i + 1), bi - b, NEG_BIG))
            kts.append((k * e).astype(BF16))
        att_all = lax.dot_general(qt, jnp.concatenate(kts, axis=0), NT_DIMS, preferred_element_type=F32)
        att = jnp.concatenate(
            [att_all[GLA_SUB * i:GLA_SUB * (i + 1), i * C:(i + 1) * C] for i in range(nsub)], axis=0)
        att = jnp.where(tril, att, 0.0).astype(BF16)
        o = o_inter + jnp.dot(att, v, preferred_element_type=F32)
        bl = b[C - 1:C, :]
        kh = (k * jnp.exp(bl - b)).astype(BF16)
        st_ref[h] = st * jnp.exp(bl) + lax.dot_general(v, kh, TN_DIMS, preferred_element_type=F32)
        y = _rms(o, gn) * r_ref[0, :, vs].astype(F32)
        o_ref[0, :, vs] = y.astype(o_ref.dtype)

    @pl.when(c == pl.num_programs(1) - 1)
    def _():
        for h in range(B_HEADS):
            sfin_ref[0, h] = st_ref[h].T


def _gla_prompt(qb, kb, vb, la, rb, gn, bsz, t):
    C = GLA_CHUNK
    blk = lambda b, c: (b, c, 0)
    o, sfin = pl.pallas_call(
        _gla_kernel,
        grid=(bsz, t // C),
        in_specs=[
            pl.BlockSpec((1, C, B_KW), blk),
            pl.BlockSpec((1, C, B_KW), blk),
            pl.BlockSpec((1, C, B_VW), blk),
            pl.BlockSpec((1, C, B_KW), blk),
            pl.BlockSpec((1, C, B_VW), blk),
            pl.BlockSpec((1, B_DV), lambda b, c: (0, 0)),
        ],
        out_specs=[pl.BlockSpec((1, C, B_VW), blk),
                   pl.BlockSpec((1, B_HEADS, B_DK, B_DV), lambda b, c: (b, 0, 0, 0))],
        out_shape=[jax.ShapeDtypeStruct((bsz, t, B_VW), BF16),
                   jax.ShapeDtypeStruct((bsz, B_HEADS, B_DK, B_DV), F32)],
        scratch_shapes=[pltpu.VMEM((B_HEADS, B_DV, B_DK), F32)],
        compiler_params=_params(("parallel", "arbitrary")),
        name="gla_prompt",
    )(qb.reshape(bsz, t, B_KW), kb.reshape(bsz, t, B_KW), vb.reshape(bsz, t, B_VW),
      la.reshape(bsz, t, B_KW), rb.reshape(bsz, t, B_VW), gn)
    return o.reshape(bsz * t, B_VW), sfin


def _merge_kernel(oa_ref, ob_ref, ga_ref, gb_ref, x_ref, wpa_ref, wpb_ref, wo_ref, g2_ref, xo_ref, h2_ref):
    ya = jnp.dot(oa_ref[...].astype(BF16), wpa_ref[...], preferred_element_type=F32)
    yb = jnp.dot(ob_ref[...].astype(BF16), wpb_ref[...], preferred_element_type=F32)
    mixed = ga_ref[...].astype(F32) * ya + gb_ref[...].astype(F32) * yb
    xn = x_ref[...] + jnp.dot(mixed.astype(BF16), wo_ref[...], preferred_element_type=F32)
    xo_ref[...] = xn
    h2_ref[...] = _rms(xn, g2_ref[...]).astype(h2_ref.dtype)


def _merge(oa, ob, ga, gb, x, wpa, wpb, wo, g2, tm):
    n = x.shape[0]
    row = lambda i: (i, 0)
    const = lambda i: (0, 0)
    return pl.pallas_call(
        _merge_kernel,
        grid=(n // tm,),
        in_specs=[
            pl.BlockSpec((tm, A_OUT), row),
            pl.BlockSpec((tm, B_VW), row),
            pl.BlockSpec((tm, D_MODEL), row),
            pl.BlockSpec((tm, D_MODEL), row),
            pl.BlockSpec((tm, D_MODEL), row),
            pl.BlockSpec((A_OUT, D_MODEL), const),
            pl.BlockSpec((B_VW, D_MODEL), const),
            pl.BlockSpec((D_MODEL, D_MODEL), const),
            pl.BlockSpec((1, D_MODEL), const),
        ],
        out_specs=[pl.BlockSpec((tm, D_MODEL), row), pl.BlockSpec((tm, D_MODEL), row)],
        out_shape=[jax.ShapeDtypeStruct((n, D_MODEL), F32), jax.ShapeDtypeStruct((n, D_MODEL), BF16)],
        compiler_params=_params(("parallel",)),
        name="merge",
    )(oa, ob, ga, gb, x, wpa, wpb, wo, g2)


def _ffn_kernel(h_ref, x_ref, w1_ref, w2_ref, o_ref):
    @pl.when(pl.program_id(1) == 0)
    def _():
        o_ref[...] = x_ref[...]

    a = jnp.maximum(jnp.dot(h_ref[...], w1_ref[...], preferred_element_type=F32), 0.0)
    o_ref[...] += jnp.dot((a * a).astype(BF16), w2_ref[...], preferred_element_type=F32)


def _ffn(h2, x, w1, w2, tm, tf=1024):
    n = x.shape[0]
    return pl.pallas_call(
        _ffn_kernel,
        grid=(n // tm, D_FF // tf),
        in_specs=[
            pl.BlockSpec((tm, D_MODEL), lambda i, k: (i, 0)),
            pl.BlockSpec((tm, D_MODEL), lambda i, k: (i, 0)),
            pl.BlockSpec((D_MODEL, tf), lambda i, k: (0, k)),
            pl.BlockSpec((tf, D_MODEL), lambda i, k: (k, 0)),
        ],
        out_specs=pl.BlockSpec((tm, D_MODEL), lambda i, k: (i, 0)),
        out_shape=jax.ShapeDtypeStruct((n, D_MODEL), F32),
        compiler_params=_params(("parallel", "arbitrary")),
        name="ffn",
    )(h2, x, w1, w2)


def _final_norm_kernel(x_ref, g_ref, o_ref):
    o_ref[...] = _rms(x_ref[...], g_ref[...])


def _final_norm(x, g, tm):
    n = x.shape[0]
    return pl.pallas_call(
        _final_norm_kernel,
        grid=(n // tm,),
        in_specs=[pl.BlockSpec((tm, D_MODEL), lambda i: (i, 0)), pl.BlockSpec((1, D_MODEL), lambda i: (0, 0))],
        out_specs=pl.BlockSpec((tm, D_MODEL), lambda i: (i, 0)),
        out_shape=jax.ShapeDtypeStruct((n, D_MODEL), F32),
        compiler_params=_params(("parallel",)),
        name="final_norm",
    )(x, g)


def _sattn_kernel(q_ref, k_ref, v_ref, c1_ref, c2_ref, c3_ref, o_ref, *, bt):
    rows = 8
    head_of_lane = lax.broadcasted_iota(jnp.int32, (rows, A_OUT), 1) >> (A_HEAD_DIM.bit_length() - 1)
    own = head_of_lane == lax.broadcasted_iota(jnp.int32, (rows, A_OUT), 0)
    for b in range(bt):
        outs, lses = [], []
        tok = pl.ds(pl.program_id(0) * bt + b, 1)
        for g, (c_ref, (_, dil)) in enumerate(zip((c1_ref, c2_ref, c3_ref), A_GROUPS)):
            gs = slice(g * A_OUT, (g + 1) * A_OUT)
            q = jnp.where(own, q_ref[tok, gs], 0.0)
            buf_len = c_ref.shape[-1]
            kc = c_ref[0, b, 0].reshape(A_OUT, buf_len).astype(BF16)
            vc = c_ref[0, b, 1].reshape(A_OUT, buf_len).astype(BF16)
            s = jnp.dot(q.astype(BF16), kc, preferred_element_type=F32)
            pos = lax.broadcasted_iota(jnp.int32, (rows, buf_len), 1)
            s = jnp.where((pos & (dil - 1)) == 0, s, NEG_BIG)
            sn = jnp.sum(q * k_ref[tok, gs], axis=1, keepdims=True)
            m = jnp.maximum(jnp.max(s, axis=1, keepdims=True), sn)
            p = jnp.exp(s - m)
            pn = jnp.exp(sn - m)
            l = jnp.sum(p, axis=1, keepdims=True) + pn
            pv = lax.dot_general(p.astype(BF16), vc, NT_DIMS, preferred_element_type=F32)
            outs.append(jnp.where(own, pv + pn * v_ref[tok, gs], 0.0) / l)
            lses.append(m + jnp.log(l))
        m = jnp.maximum(jnp.maximum(lses[0], lses[1]), lses[2])
        es = [jnp.exp(x - m) for x in lses]
        o = (es[0] * outs[0] + es[1] * outs[1] + es[2] * outs[2]) / (es[0] + es[1] + es[2])
        o_ref[tok, :] = jnp.sum(o, axis=0, keepdims=True)


def _sattn(qa, ka, va, caches, layer, bt=2):
    nb = qa.shape[0]
    views, cspecs = [], []
    for c in caches:
        v = jnp.transpose(c, (0, 1, 3, 4, 5, 2))
        views.append(v)
        cspecs.append(pl.BlockSpec((1, bt) + v.shape[2:], lambda i: (layer, i, 0, 0, 0, 0)))
    whole = lambda i: (0, 0)
    return pl.pallas_call(
        functools.partial(_sattn_kernel, bt=bt),
        grid=(nb // bt,),
        in_specs=[pl.BlockSpec((nb, A_QKV), whole)] * 3 + cspecs,
        out_specs=pl.BlockSpec((nb, A_OUT), whole),
        out_shape=jax.ShapeDtypeStruct((nb, A_OUT), F32),
        compiler_params=_params(("arbitrary",)),
        name="attn_sample",
    )(qa, ka, va, *views)


def _sgla_kernel(q_ref, k_ref, v_ref, la_ref, r_ref, gn_ref, s_ref, o_ref, so_ref, oraw, *, bt):
    a = jnp.exp(la_ref[...])
    q = q_ref[...].astype(F32)
    k = k_ref[...].astype(F32)
    v = v_ref[...].astype(F32)

    def column(arr, b, h):
        r = arr[b:b + 1, h * B_DK:(h + 1) * B_DK]
        return jnp.broadcast_to(r, (LANES, B_DK)).T

    for b in range(bt):
        for h in range(B_HEADS):
            ac, kc, qc = column(a, b, h), column(k, b, h), column(q, b, h)
            for half in range(B_DV // LANES):
                ls = slice(half * LANES, (half + 1) * LANES)
                vh = v[b:b + 1, h * B_DV + half * LANES:h * B_DV + (half + 1) * LANES]
                sn = ac * s_ref[0, b, h, :, ls] + kc * vh
                so_ref[b, h, :, ls] = sn
                oraw[b:b + 1, h * B_DV + half * LANES:h * B_DV + (half + 1) * LANES] = jnp.sum(
                    qc * sn, axis=0, keepdims=True)
    gn = gn_ref[...]
    for h in range(B_HEADS):
        vs = slice(h * B_DV, (h + 1) * B_DV)
        o_ref[:, vs] = _rms(oraw[:, vs], gn) * r_ref[:, vs].astype(F32)


def _sgla(qb, kb, vb, la, rb, gn, state, layer, bt=8):
    nb = qb.shape[0]
    row = lambda i: (i, 0)
    sblk = (1, bt, B_HEADS, B_DK, B_DV)
    return pl.pallas_call(
        functools.partial(_sgla_kernel, bt=bt),
        grid=(nb // bt,),
        in_specs=[
            pl.BlockSpec((bt, B_KW), row),
            pl.BlockSpec((bt, B_KW), row),
            pl.BlockSpec((bt, B_VW), row),
            pl.BlockSpec((bt, B_KW), row),
            pl.BlockSpec((bt, B_VW), row),
            pl.BlockSpec((1, B_DV), lambda i: (0, 0)),
            pl.BlockSpec(sblk, lambda i: (layer, i, 0, 0, 0)),
        ],
        out_specs=[pl.BlockSpec((bt, B_VW), row), pl.BlockSpec(sblk[1:], lambda i: (i, 0, 0, 0))],
        out_shape=[jax.ShapeDtypeStruct((nb, B_VW), F32),
                   jax.ShapeDtypeStruct((nb, B_HEADS, B_DK, B_DV), F32)],
        scratch_shapes=[pltpu.VMEM((bt, B_VW), F32)],
        compiler_params=_params(("parallel",)),
        name="gla_sample",
    )(qb, kb, vb, la, rb, gn, state)


def _rope_tables(pos):
    half = A_ROT_DIM // 2
    inv_freq = 1.0 / (ROPE_THETA ** (jnp.arange(half, dtype=F32) / half))
    ang = pos.astype(F32)[:, None] * inv_freq[None, :]
    cos, sin = jnp.cos(ang), jnp.sin(ang)
    n = pos.shape[0]
    pad = A_HEAD_DIM - A_ROT_DIM
    cos_h = jnp.concatenate([cos, cos, jnp.ones((n, pad), F32)], axis=1)
    sa_h = jnp.concatenate([-sin, jnp.zeros((n, half + pad), F32)], axis=1)
    sb_h = jnp.concatenate([jnp.zeros((n, half), F32), sin, jnp.zeros((n, pad), F32)], axis=1)
    rep = LANES // A_HEAD_DIM
    return jnp.tile(cos_h, (1, rep)), jnp.tile(sa_h, (1, rep)), jnp.tile(sb_h, (1, rep))


def _kv_rows(k, v):
    k = k.astype(F32).reshape(k.shape[:-1] + (A_HEADS, A_HEAD_DIM))
    v = v.astype(F32).reshape(v.shape[:-1] + (A_HEADS, A_HEAD_DIM))
    return jnp.stack([k, v], axis=-3)


def _last_tokens(x, keep):
    bsz, dil, n, w = x.shape
    rows = keep // dil
    return jnp.swapaxes(x[:, :, n - rows:, :], 1, 2).reshape(bsz, keep, w)


def kernel(x_prompt, x_sample, cache_a1_kv, cache_a2_kv, cache_a3_kv, state_gla, norm1_g, w_in, w_gate_up, b_gate,
           gla_norm_g, w_pa, w_pb, w_o, norm2_g, w_ff1, w_ff2, final_norm_g):
    bsz, t, _ = x_prompt.shape
    nb = x_sample.shape[0]
    depth = w_in.shape[0]
    assert x_sample.shape[1] == 1 and t % A_TOKENS == 0 and t % GLA_CHUNK == 0
    caches = (cache_a1_kv, cache_a2_kv, cache_a3_kv)
    for c, (window, _) in zip(caches, A_GROUPS):
        assert c.shape[2] == window

    gl0, gl1 = _SEG["gl"][0], _SEG["gl"][0] + B_GATE_RANK
    w_in_p = jnp.concatenate(
        [w_in[:, :, :gl0], w_in[:, :, gl0:gl1], jnp.zeros((depth, D_MODEL, GL_PAD - B_GATE_RANK), w_in.dtype),
         w_in[:, :, gl1:]], axis=2).astype(BF16)
    wgu_p = jnp.concatenate([w_gate_up, jnp.zeros((depth, GL_PAD - B_GATE_RANK, B_KW), F32)], axis=1)
    wgu_hi = wgu_p.astype(BF16)
    wgu_lo = (wgu_p - wgu_hi.astype(F32)).astype(BF16)
    w_pa_b, w_pb_b, w_o_b = w_pa.astype(BF16), w_pb.astype(BF16), w_o.astype(BF16)
    w_ff1_b, w_ff2_b = w_ff1.astype(BF16), w_ff2.astype(BF16)

    tm_p = 512
    tabs_p = _rope_tables(jnp.arange(t, dtype=jnp.int32))
    tabs_s = _rope_tables(jnp.full((nb,), PAST_LEN, dtype=jnp.int32))

    xp = x_prompt.reshape(bsz * t, D_MODEL)
    xs = x_sample.reshape(nb, D_MODEL)
    p_kv = [[], [], []]
    s_kv = [[], [], []]
    p_st, s_st = [], []
    for l in range(depth):
        g1 = norm1_g[l][None]
        g2 = norm2_g[l][None]
        bg = b_gate[l][None]
        gn = gla_norm_g[l][None]

        outs = _inproj(xp, g1, w_in_p[l], tabs_p, wgu_hi[l], wgu_lo[l], bg, tm_p, seq_len=t)
        qkv, (qb, kb, vb, la, rb, ga, gb) = outs[:3 * N_GROUPS], outs[3 * N_GROUPS:]
        oa = _attn_prompt(qkv, bsz, t)
        ob, st_p = _gla_prompt(qb, kb, vb, la, rb, gn, bsz, t)
        xp, h2 = _merge(oa, ob, ga, gb, xp, w_pa_b[l], w_pb_b[l], w_o_b[l], g2, tm_p)
        xp = _ffn(h2, xp, w_ff1_b[l], w_ff2_b[l], 1024)
        for g, (window, _) in enumerate(A_GROUPS):
            keep = min(window, t)
            p_kv[g].append(_kv_rows(_last_tokens(qkv[N_GROUPS + g], keep), _last_tokens(qkv[2 * N_GROUPS + g], keep)))
        p_st.append(st_p)

        qa, ka, va, qb, kb, vb, la, rb, ga, gb = _inproj(xs, g1, w_in_p[l], tabs_s, wgu_hi[l], wgu_lo[l], bg, nb)
        oa = _sattn(qa, ka, va, caches, l)
        ob, st_s = _sgla(qb, kb, vb, la, rb, gn, state_gla, l)
        xs, h2 = _merge(oa, ob, ga, gb, xs, w_pa_b[l], w_pb_b[l], w_o_b[l], g2, nb)
        xs = _ffn(h2, xs, w_ff1_b[l], w_ff2_b[l], nb)
        for g in range(N_GROUPS):
            gs = slice(g * A_OUT, (g + 1) * A_OUT)
            s_kv[g].append(_kv_rows(ka[:, None, gs], va[:, None, gs]))
        s_st.append(st_s)

    fg = final_norm_g[None]
    y_prompt = _final_norm(xp, fg, tm_p).reshape(bsz, t, D_MODEL)
    y_sample = _final_norm(xs, fg, nb).reshape(nb, 1, D_MODEL)
    return (y_prompt, y_sample,
            jnp.stack(p_kv[0]), jnp.stack(p_kv[1]), jnp.stack(p_kv[2]), jnp.stack(p_st),
            jnp.stack(s_kv[0]), jnp.stack(s_kv[1]), jnp.stack(s_kv[2]), jnp.stack(s_st))
```

```python
import functools

import jax
import jax.numpy as jnp
from jax import lax
from jax.experimental import pallas as pl
from jax.experimental.pallas import tpu as pltpu

F32 = jnp.float32
BF16 = jnp.bfloat16

D_MODEL = 1024
PAST_LEN = 2048
A_GROUPS = ((128, 1), (512, 4), (2048, 16))
N_GROUPS = 3
A_HEADS = 4
A_HEAD_DIM = 64
A_ROT_DIM = A_HEAD_DIM // 4
ROPE_THETA = 500000.0
A_QKV = N_GROUPS * A_HEADS * A_HEAD_DIM
A_OUT = A_HEADS * A_HEAD_DIM
A_SPAN = 128
A_TOKENS = A_GROUPS[-1][1] * A_SPAN

B_HEADS = 4
B_DK = 128
B_DV = 256
B_KW = B_HEADS * B_DK
B_VW = B_HEADS * B_DV
B_GATE_RANK = 16
B_GATE_TAU = 16.0
D_FF = 4 * D_MODEL
EPS = 1e-6

LANES = 128
GLA_CHUNK = 128
GLA_SUB = 16
ATTN_UNROLL_FIRST = 4
ATTN_UNROLL_LATER = 5
NEG_BIG = -1e30
VMEM_LIMIT = 56 * 1024 * 1024

GL_PAD = LANES
_SEG_SIZES = (A_QKV, A_QKV, A_QKV, B_KW, B_KW, B_VW, GL_PAD, B_VW, D_MODEL, D_MODEL)
_SEG_NAMES = ("qa", "ka", "va", "qb", "kb", "vb", "gl", "rb", "ga", "gb")
_SEG = {}
_off = 0
for _n, _s in zip(_SEG_NAMES, _SEG_SIZES):
    _SEG[_n] = (_off, _off + _s)
    _off += _s
IN_WIDTH_PAD = _off

NT_DIMS = (((1,), (1,)), ((), ()))
TN_DIMS = (((0,), (0,)), ((), ()))


def _params(sem):
    return pltpu.CompilerParams(dimension_semantics=sem, vmem_limit_bytes=VMEM_LIMIT)


def _sigmoid(x):
    return 1.0 / (1.0 + jnp.exp(-x))


def _log_sigmoid(x):
    return jnp.minimum(x, 0.0) - jnp.log(1.0 + jnp.exp(-jnp.abs(x)))


def _rms(x, g):
    ms = jnp.mean(x * x, axis=-1, keepdims=True)
    return x * lax.rsqrt(ms + EPS) * g


def _split_bf16(x):
    hi = x.astype(BF16)
    return hi, (x - hi.astype(F32)).astype(BF16)


def _win_prep_kernel(w_ref, o_ref):
    gl0 = _SEG["gl"][0]
    shift = GL_PAD - B_GATE_RANK
    o_ref[:, 0:gl0] = w_ref[:, 0:gl0].astype(o_ref.dtype)
    lane = lax.broadcasted_iota(jnp.int32, (w_ref.shape[0], LANES), 1)
    o_ref[:, gl0:gl0 + LANES] = jnp.where(lane < B_GATE_RANK, w_ref[:, gl0:gl0 + LANES], 0.0).astype(o_ref.dtype)
    for j in range((IN_WIDTH_PAD - gl0 - LANES) // LANES):
        a = gl0 + j * LANES
        tile = jnp.concatenate([w_ref[:, a:a + LANES][:, B_GATE_RANK:], w_ref[:, a + LANES:a + LANES + B_GATE_RANK]],
                               axis=1)
        o_ref[:, a + shift + B_GATE_RANK:a + shift + B_GATE_RANK + LANES] = tile.astype(o_ref.dtype)


def _win_prep(w_in, tr=256):
    depth, d, width = w_in.shape
    assert width + GL_PAD - B_GATE_RANK == IN_WIDTH_PAD
    return pl.pallas_call(
        _win_prep_kernel,
        grid=(depth, d // tr),
        in_specs=[pl.BlockSpec((None, tr, width), lambda l, i: (l, i, 0))],
        out_specs=pl.BlockSpec((None, tr, IN_WIDTH_PAD), lambda l, i: (l, i, 0)),
        out_shape=jax.ShapeDtypeStruct((depth, d, IN_WIDTH_PAD), BF16),
        compiler_params=_params(("parallel", "parallel")),
        name="win_prep",
    )(w_in)


def _inproj_kernel(x_ref, g_ref, w_ref, cos_ref, sa_ref, sb_ref, wgh_ref, wgl_ref, bg_ref, *refs, dilate):
    n_attn = 3 * N_GROUPS if dilate else 3
    attn_refs = refs[:n_attn]
    qb_ref, kb_ref, vb_ref, la_ref, rb_ref, ga_ref, gb_ref = refs[n_attn:n_attn + 7]
    scratch = refs[n_attn + 7:]
    tm = x_ref.shape[0]
    h = _rms(x_ref[...], g_ref[...]).astype(BF16)

    def seg(name):
        c0, c1 = _SEG[name]
        return jnp.dot(h, w_ref[:, c0:c1], preferred_element_type=F32)

    cosl, sa, sb = cos_ref[...], sa_ref[...], sb_ref[...]

    def rope(xc):
        return xc * cosl + pltpu.roll(xc, LANES - A_ROT_DIM // 2, 1) * sa + pltpu.roll(xc, A_ROT_DIM // 2, 1) * sb

    def attn_store(name, which, scale, rotate):
        acc = seg(name)
        tiles_per_group = A_OUT // LANES
        for g, (_, dil) in enumerate(A_GROUPS):
            for c in range(tiles_per_group):
                cg = g * tiles_per_group + c
                xc = acc[:, cg * LANES:(cg + 1) * LANES]
                if rotate:
                    xc = rope(xc)
                if scale != 1.0:
                    xc = xc * scale
                if not dilate:
                    attn_refs[which][:, cg * LANES:(cg + 1) * LANES] = xc.astype(attn_refs[which].dtype)
                    continue
                out_ref = attn_refs[which * N_GROUPS + g]
                ls = slice(c * LANES, (c + 1) * LANES)
                if dil == 1:
                    out_ref[0, 0, :, ls] = xc.astype(out_ref.dtype)
                else:
                    dscr = scratch[0]
                    dscr[...] = xc
                    for r in range(dil):
                        out_ref[0, r, :, ls] = dscr[pl.ds(r, tm // dil, stride=dil), :].astype(out_ref.dtype)

    attn_store("qa", 0, A_HEAD_DIM ** -0.5, True)
    attn_store("ka", 1, 1.0, True)
    attn_store("va", 2, 1.0, False)
    qb_ref[...] = (seg("qb") * (B_DK ** -0.5)).astype(qb_ref.dtype)
    kb_ref[...] = seg("kb").astype(kb_ref.dtype)
    vb_ref[...] = seg("vb").astype(vb_ref.dtype)
    gh, gl = _split_bf16(seg("gl"))
    wgh, wgl = wgh_ref[...], wgl_ref[...]
    xg = (jnp.dot(gh, wgh, preferred_element_type=F32) + jnp.dot(gl, wgh, preferred_element_type=F32)
          + jnp.dot(gh, wgl, preferred_element_type=F32) + bg_ref[...])
    la_ref[...] = _log_sigmoid(xg) * (1.0 / B_GATE_TAU)
    r = seg("rb")
    rb_ref[...] = (r * _sigmoid(r)).astype(rb_ref.dtype)
    ga_ref[...] = _sigmoid(seg("ga")).astype(ga_ref.dtype)
    gb_ref[...] = _sigmoid(seg("gb")).astype(gb_ref.dtype)


def _inproj(x, g1, w, tabs, wgh, wgl, bg, layer, tm, seq_len=None):
    n = x.shape[0]
    dilate = seq_len is not None
    row = lambda i: (i, 0)
    const = lambda i: (0, 0)
    of_layer = lambda i: (layer, 0, 0)
    if dilate:
        tpb = seq_len // tm
        tab = lambda i: (i % tpb, 0)
        attn_shapes, attn_specs = [], []
        for _ in range(3):
            for _, dil in A_GROUPS:
                attn_shapes.append(jax.ShapeDtypeStruct((n // seq_len, dil, seq_len // dil, A_OUT), BF16))
                attn_specs.append(pl.BlockSpec((1, dil, tm // dil, A_OUT), lambda i: (i // tpb, 0, i % tpb, 0)))
        scratch = [pltpu.VMEM((tm, LANES), F32)]
    else:
        tab = row
        attn_shapes = [jax.ShapeDtypeStruct((n, A_QKV), F32)] * 3
        attn_specs = [pl.BlockSpec((tm, A_QKV), row)] * 3
        scratch = []
    rest = (("qb", B_KW, BF16), ("kb", B_KW, BF16), ("vb", B_VW, BF16), ("la", B_KW, F32),
            ("rb", B_VW, BF16), ("ga", D_MODEL, BF16), ("gb", D_MODEL, BF16))
    out_shape = attn_shapes + [jax.ShapeDtypeStruct((n, wd), dt) for _, wd, dt in rest]
    out_specs = attn_specs + [pl.BlockSpec((tm, wd), row) for _, wd, _ in rest]
    return pl.pallas_call(
        functools.partial(_inproj_kernel, dilate=dilate),
        grid=(n // tm,),
        in_specs=[
            pl.BlockSpec((tm, D_MODEL), row),
            pl.BlockSpec((1, D_MODEL), const),
            pl.BlockSpec((None, D_MODEL, IN_WIDTH_PAD), of_layer, pipeline_mode=pl.Buffered(1)),
            pl.BlockSpec((tm, LANES), tab),
            pl.BlockSpec((tm, LANES), tab),
            pl.BlockSpec((tm, LANES), tab),
            pl.BlockSpec((None, GL_PAD, B_KW), of_layer),
            pl.BlockSpec((None, GL_PAD, B_KW), of_layer),
            pl.BlockSpec((1, B_KW), const),
        ],
        out_specs=out_specs,
        out_shape=out_shape,
        scratch_shapes=scratch,
        compiler_params=_params(("parallel",)),
        name="inproj_prompt" if dilate else "inproj_sample",
    )(x, g1, w, *tabs, wgh, wgl, bg)


def _attn_pair(qp, kw, vw, mask2, low):
    zero = jnp.zeros_like(qp)
    q2 = jnp.concatenate([jnp.where(low, qp, zero), jnp.where(low, zero, qp)], axis=0)
    s = lax.dot_general(q2, kw, NT_DIMS, preferred_element_type=F32)
    s = jnp.where(mask2, s, NEG_BIG)
    m = jnp.max(s, axis=1, keepdims=True)
    p = jnp.exp(s - m)
    l = jnp.sum(p, axis=1, keepdims=True)
    pv = jnp.dot(p.astype(BF16), vw, preferred_element_type=F32)
    o2 = pv / l
    lse2 = m + jnp.log(l)
    o = jnp.where(low, o2[0:A_SPAN], o2[A_SPAN:])
    lse = jnp.where(low, jnp.broadcast_to(lse2[0:A_SPAN], (A_SPAN, LANES)),
                    jnp.broadcast_to(lse2[A_SPAN:], (A_SPAN, LANES)))
    return o, lse


def _attn_kernel(*refs):
    ins = refs[:5 * N_GROUPS]
    o_ref = refs[5 * N_GROUPS]
    mx, sx, nx, ro, rl = refs[5 * N_GROUPS + 1:]
    first = pl.program_id(1) == 0
    row = lax.broadcasted_iota(jnp.int32, (2 * A_SPAN, 2 * A_SPAN), 0) & (A_SPAN - 1)
    col = lax.broadcasted_iota(jnp.int32, (2 * A_SPAN, 2 * A_SPAN), 1)
    band = (col >= row) & (col <= row + A_SPAN)
    band_first = band & (col >= jnp.where(first, A_SPAN, 0))
    low = lax.broadcasted_iota(jnp.int32, (A_SPAN, LANES), 1) < A_HEAD_DIM
    pairs = A_HEADS // 2

    for g, (_, dil) in enumerate(A_GROUPS):
        q_ref, kp_ref, kc_ref, vp_ref, vc_ref = ins[5 * g:5 * g + 5]
        nrow = A_TOKENS // dil
        nj = nrow // A_SPAN

        def emit(r, j, o, lse, hp):
            r0 = pl.multiple_of(j * A_SPAN, A_SPAN)
            if dil == 1:
                mx[hp, pl.ds(r0, A_SPAN), :] = lse
                sx[hp, pl.ds(r0, A_SPAN), :] = jnp.ones_like(lse)
                nx[hp, pl.ds(r0, A_SPAN), :] = o
            else:
                base = pl.multiple_of(r * nrow + r0, A_SPAN)
                ro[hp, pl.ds(base, A_SPAN), :] = o
                rl[hp, pl.ds(base, A_SPAN), :] = lse

        def first_block(r, carry):
            for hp in range(pairs):
                cs = slice(hp * LANES, (hp + 1) * LANES)
                kw = jnp.concatenate([kp_ref[0, r, :, cs], kc_ref[0, r, 0:A_SPAN, cs]], axis=0)
                vw = jnp.concatenate([vp_ref[0, r, :, cs], vc_ref[0, r, 0:A_SPAN, cs]], axis=0)
                o, lse = _attn_pair(q_ref[0, r, 0:A_SPAN, cs], kw, vw, band_first, low)
                emit(r, 0, o, lse, hp)
            return carry

        lax.fori_loop(0, dil, first_block, 0, unroll=min(dil, ATTN_UNROLL_FIRST))

        if nj > 1:
            def later_blocks(r, carry):
                def later_block(j, inner):
                    k0 = pl.multiple_of((j - 1) * A_SPAN, A_SPAN)
                    q0 = pl.multiple_of(j * A_SPAN, A_SPAN)
                    for hp in range(pairs):
                        cs = slice(hp * LANES, (hp + 1) * LANES)
                        o, lse = _attn_pair(q_ref[0, r, pl.ds(q0, A_SPAN), cs],
                                            kc_ref[0, r, pl.ds(k0, 2 * A_SPAN), cs],
                                            vc_ref[0, r, pl.ds(k0, 2 * A_SPAN), cs], band, low)
                        emit(r, j, o, lse, hp)
                    return inner

                return lax.fori_loop(1, nj, later_block, carry, unroll=min(nj - 1, ATTN_UNROLL_LATER))

            lax.fori_loop(0, dil, later_blocks, 0)

        if dil > 1:
            for hp in range(pairs):
                for r in range(dil):
                    rows = pl.ds(r, nrow, stride=dil)
                    lse = rl[hp, r * nrow:(r + 1) * nrow, :]
                    m_old = mx[hp, rows, :]
                    m_new = jnp.maximum(m_old, lse)
                    a_old = jnp.exp(m_old - m_new)
                    a_new = jnp.exp(lse - m_new)
                    mx[hp, rows, :] = m_new
                    sx[hp, rows, :] = sx[hp, rows, :] * a_old + a_new
                    nx[hp, rows, :] = nx[hp, rows, :] * a_old + a_new * ro[hp, r * nrow:(r + 1) * nrow, :]

    for hp in range(pairs):
        o_ref[0, :, hp * LANES:(hp + 1) * LANES] = (nx[hp] / sx[hp]).astype(o_ref.dtype)


def _attn_prompt(qkv, bsz, t):
    in_specs, args = [], []
    for g, (_, dil) in enumerate(A_GROUPS):
        nrow = A_TOKENS // dil
        sub = nrow // A_SPAN
        cur = pl.BlockSpec((1, dil, nrow, A_OUT), lambda b, i: (b, 0, i, 0))
        prev = pl.BlockSpec((1, dil, A_SPAN, A_OUT), lambda b, i, sub=sub: (b, 0, jnp.maximum(i * sub - 1, 0), 0))
        q, k, v = qkv[g], qkv[N_GROUPS + g], qkv[2 * N_GROUPS + g]
        in_specs += [cur, prev, cur, prev, cur]
        args += [q, k, k, v, v]
    o = pl.pallas_call(
        _attn_kernel,
        grid=(bsz, t // A_TOKENS),
        in_specs=in_specs,
        out_specs=pl.BlockSpec((1, A_TOKENS, A_OUT), lambda b, i: (b, i, 0)),
        out_shape=jax.ShapeDtypeStruct((bsz, t, A_OUT), BF16),
        scratch_shapes=[pltpu.VMEM((A_OUT // LANES, A_TOKENS, LANES), F32)] * 5,
        compiler_params=_params(("parallel", "parallel")),
        name="attn_prompt",
    )(*args)
    return o.reshape(bsz * t, A_OUT)


def _gla_kernel(q_ref, k_ref, v_ref, la_ref, r_ref, gn_ref, o_ref, sfin_ref, st_ref):
    c = pl.program_id(0)
    C = GLA_CHUNK
    nsub = C // GLA_SUB
    nbatch = q_ref.shape[0]

    @pl.when(c == 0)
    def _():
        st_ref[...] = jnp.zeros_like(st_ref)

    ri = lax.broadcasted_iota(jnp.int32, (C, C), 0)
    ci = lax.broadcasted_iota(jnp.int32, (C, C), 1)
    tril = ci <= ri
    tril_b = jnp.where(tril, 1.0, 0.0).astype(BF16)
    srow = lax.broadcasted_iota(jnp.int32, (C, B_DK), 0)
    gn = gn_ref[...]

    for bb in range(nbatch):
        la_hi, la_lo = _split_bf16(la_ref[bb])
        b_all = (jnp.dot(tril_b, la_hi, preferred_element_type=F32)
                 + jnp.dot(tril_b, la_lo, preferred_element_type=F32))
        for h in range(B_HEADS):
            ks = slice(h * B_DK, (h + 1) * B_DK)
            vs = slice(h * B_DV, (h + 1) * B_DV)
            si = bb * B_HEADS + h
            b = b_all[:, ks]
            q = q_ref[bb, :, ks].astype(F32)
            k = k_ref[bb, :, ks].astype(F32)
            v = v_ref[bb, :, vs]
            st = st_ref[si]
            o_inter = lax.dot_general((q * jnp.exp(b)).astype(BF16), st.astype(BF16), NT_DIMS,
                                      preferred_element_type=F32)
            beta = b.reshape(nsub, GLA_SUB, B_DK)[:, 0:1, :]
            beta_rows = jnp.broadcast_to(beta, (nsub, GLA_SUB, B_DK)).reshape(C, B_DK)
            qt = (q * jnp.exp(b - beta_rows)).astype(BF16)
            kts = []
            for i in range(nsub):
                bi = b[GLA_SUB * i:GLA_SUB * i + 1, :]
                e = jnp.exp(jnp.where(srow < GLA_SUB * (i + 1), bi - b, NEG_BIG))
                kts.append((k * e).astype(BF16))
            att_all = lax.dot_general(qt, jnp.concatenate(kts, axis=0), NT_DIMS, preferred_element_type=F32)
            att = jnp.concatenate(
                [att_all[GLA_SUB * i:GLA_SUB * (i + 1), i * C:(i + 1) * C] for i in range(nsub)], axis=0)
            att = jnp.where(tril, att, 0.0).astype(BF16)
            o = o_inter + jnp.dot(att, v, preferred_element_type=F32)
            bl = b[C - 1:C, :]
            kh = (k * jnp.exp(bl - b)).astype(BF16)
            st_ref[si] = st * jnp.exp(bl) + lax.dot_general(v, kh, TN_DIMS, preferred_element_type=F32)
            y = _rms(o, gn) * r_ref[bb, :, vs].astype(F32)
            o_ref[bb, :, vs] = y.astype(o_ref.dtype)

    @pl.when(c == pl.num_programs(0) - 1)
    def _():
        for bb in range(nbatch):
            for h in range(B_HEADS):
                sfin_ref[bb, h] = st_ref[bb * B_HEADS + h].T


def _gla_prompt(qb, kb, vb, la, rb, gn, bsz, t):
    C = GLA_CHUNK
    blk = lambda c: (0, c, 0)
    o, sfin = pl.pallas_call(
        _gla_kernel,
        grid=(t // C,),
        in_specs=[
            pl.BlockSpec((bsz, C, B_KW), blk),
            pl.BlockSpec((bsz, C, B_KW), blk),
            pl.BlockSpec((bsz, C, B_VW), blk),
            pl.BlockSpec((bsz, C, B_KW), blk),
            pl.BlockSpec((bsz, C, B_VW), blk),
            pl.BlockSpec((1, B_DV), lambda c: (0, 0)),
        ],
        out_specs=[pl.BlockSpec((bsz, C, B_VW), blk),
                   pl.BlockSpec((bsz, B_HEADS, B_DK, B_DV), lambda c: (0, 0, 0, 0))],
        out_shape=[jax.ShapeDtypeStruct((bsz, t, B_VW), BF16),
                   jax.ShapeDtypeStruct((bsz, B_HEADS, B_DK, B_DV), F32)],
        scratch_shapes=[pltpu.VMEM((bsz * B_HEADS, B_DV, B_DK), F32)],
        compiler_params=_params(("arbitrary",)),
        name="gla_prompt",
    )(qb.reshape(bsz, t, B_KW), kb.reshape(bsz, t, B_KW), vb.reshape(bsz, t, B_VW),
      la.reshape(bsz, t, B_KW), rb.reshape(bsz, t, B_VW), gn)
    return o.reshape(bsz * t, B_VW), sfin


def _merge_kernel(oa_ref, ob_ref, ga_ref, gb_ref, x_ref, wpa_ref, wpb_ref, wo_ref, g2_ref, xo_ref, h2_ref):
    ya = jnp.dot(oa_ref[...].astype(BF16), wpa_ref[...], preferred_element_type=F32)
    yb = jnp.dot(ob_ref[...].astype(BF16), wpb_ref[...], preferred_element_type=F32)
    mixed = ga_ref[...].astype(F32) * ya + gb_ref[...].astype(F32) * yb
    xn = x_ref[...] + jnp.dot(mixed.astype(BF16), wo_ref[...], preferred_element_type=F32)
    xo_ref[...] = xn
    h2_ref[...] = _rms(xn, g2_ref[...]).astype(h2_ref.dtype)


def _merge(oa, ob, ga, gb, x, wpa, wpb, wo, g2, layer, tm):
    n = x.shape[0]
    row = lambda i: (i, 0)
    const = lambda i: (0, 0)
    of_layer = lambda i: (layer, 0, 0)
    return pl.pallas_call(
        _merge_kernel,
        grid=(n // tm,),
        in_specs=[
            pl.BlockSpec((tm, A_OUT), row),
            pl.BlockSpec((tm, B_VW), row),
            pl.BlockSpec((tm, D_MODEL), row),
            pl.BlockSpec((tm, D_MODEL), row),
            pl.BlockSpec((tm, D_MODEL), row),
            pl.BlockSpec((None, A_OUT, D_MODEL), of_layer),
            pl.BlockSpec((None, B_VW, D_MODEL), of_layer),
            pl.BlockSpec((None, D_MODEL, D_MODEL), of_layer),
            pl.BlockSpec((1, D_MODEL), const),
        ],
        out_specs=[pl.BlockSpec((tm, D_MODEL), row), pl.BlockSpec((tm, D_MODEL), row)],
        out_shape=[jax.ShapeDtypeStruct((n, D_MODEL), F32), jax.ShapeDtypeStruct((n, D_MODEL), BF16)],
        compiler_params=_params(("parallel",)),
        name="merge",
    )(oa, ob, ga, gb, x, wpa, wpb, wo, g2)


def _ffn_kernel(h_ref, x_ref, w1_ref, w2_ref, gf_ref, o_ref, *, final_norm):
    k = pl.program_id(1)

    @pl.when(k == 0)
    def _():
        o_ref[...] = x_ref[...]

    a = jnp.maximum(jnp.dot(h_ref[...], w1_ref[...], preferred_element_type=F32), 0.0)
    o_ref[...] += jnp.dot((a * a).astype(BF16), w2_ref[...], preferred_element_type=F32)

    if final_norm:
        @pl.when(k == pl.num_programs(1) - 1)
        def _():
            o_ref[...] = _rms(o_ref[...], gf_ref[...])


def _ffn(h2, x, w1, w2, gf, layer, tm, final_norm, tf=1024):
    n = x.shape[0]
    return pl.pallas_call(
        functools.partial(_ffn_kernel, final_norm=final_norm),
        grid=(n // tm, D_FF // tf),
        in_specs=[
            pl.BlockSpec((tm, D_MODEL), lambda i, k: (i, 0)),
            pl.BlockSpec((tm, D_MODEL), lambda i, k: (i, 0)),
            pl.BlockSpec((None, D_MODEL, tf), lambda i, k: (layer, 0, k)),
            pl.BlockSpec((None, tf, D_MODEL), lambda i, k: (layer, k, 0)),
            pl.BlockSpec((1, D_MODEL), lambda i, k: (0, 0)),
        ],
        out_specs=pl.BlockSpec((tm, D_MODEL), lambda i, k: (i, 0)),
        out_shape=jax.ShapeDtypeStruct((n, D_MODEL), F32),
        compiler_params=_params(("parallel", "arbitrary")),
        name="ffn",
    )(h2, x, w1, w2, gf)


def _sattn_kernel(q_ref, k_ref, v_ref, c1_ref, c2_ref, c3_ref, o_ref, *, bt):
    rows = 8
    head_of_lane = lax.broadcasted_iota(jnp.int32, (rows, A_OUT), 1) >> (A_HEAD_DIM.bit_length() - 1)
    own = head_of_lane == lax.broadcasted_iota(jnp.int32, (rows, A_OUT), 0)
    for b in range(bt):
        outs, lses = [], []
        tok = pl.ds(pl.program_id(0) * bt + b, 1)
        for g, (c_ref, (_, dil)) in enumerate(zip((c1_ref, c2_ref, c3_ref), A_GROUPS)):
            gs = slice(g * A_OUT, (g + 1) * A_OUT)
            q = jnp.where(own, q_ref[tok, gs], 0.0)
            buf_len = c_ref.shape[-1]
            kc = c_ref[0, b, 0].reshape(A_OUT, buf_len).astype(BF16)
            vc = c_ref[0, b, 1].reshape(A_OUT, buf_len).astype(BF16)
            s = jnp.dot(q.astype(BF16), kc, preferred_element_type=F32)
            pos = lax.broadcasted_iota(jnp.int32, (rows, buf_len), 1)
            s = jnp.where((pos & (dil - 1)) == 0, s, NEG_BIG)
            sn = jnp.sum(q * k_ref[tok, gs], axis=1, keepdims=True)
            m = jnp.maximum(jnp.max(s, axis=1, keepdims=True), sn)
            p = jnp.exp(s - m)
            pn = jnp.exp(sn - m)
            l = jnp.sum(p, axis=1, keepdims=True) + pn
            pv = lax.dot_general(p.astype(BF16), vc, NT_DIMS, preferred_element_type=F32)
            outs.append(jnp.where(own, pv + pn * v_ref[tok, gs], 0.0) / l)
            lses.append(m + jnp.log(l))
        m = jnp.maximum(jnp.maximum(lses[0], lses[1]), lses[2])
        es = [jnp.exp(x - m) for x in lses]
        o = (es[0] * outs[0] + es[1] * outs[1] + es[2] * outs[2]) / (es[0] + es[1] + es[2])
        o_ref[tok, :] = jnp.sum(o, axis=0, keepdims=True)


def _sattn(qa, ka, va, caches, layer, bt=2):
    nb = qa.shape[0]
    views, cspecs = [], []
    for c in caches:
        v = jnp.transpose(c, (0, 1, 3, 4, 5, 2))
        views.append(v)
        cspecs.append(pl.BlockSpec((1, bt) + v.shape[2:], lambda i: (layer, i, 0, 0, 0, 0)))
    whole = lambda i: (0, 0)
    return pl.pallas_call(
        functools.partial(_sattn_kernel, bt=bt),
        grid=(nb // bt,),
        in_specs=[pl.BlockSpec((nb, A_QKV), whole)] * 3 + cspecs,
        out_specs=pl.BlockSpec((nb, A_OUT), whole),
        out_shape=jax.ShapeDtypeStruct((nb, A_OUT), F32),
        compiler_params=_params(("arbitrary",)),
        name="attn_sample",
    )(qa, ka, va, *views)


def _sgla_kernel(q_ref, k_ref, v_ref, la_ref, r_ref, gn_ref, s_ref, *rest, bt):
    o_ref, so_ref, oraw = rest[-3:]
    a = jnp.exp(la_ref[...])
    q = q_ref[...].astype(F32)
    k = k_ref[...].astype(F32)
    v = v_ref[...].astype(F32)

    def column(arr, b, h):
        r = arr[b:b + 1, h * B_DK:(h + 1) * B_DK]
        return jnp.broadcast_to(r, (LANES, B_DK)).T

    for b in range(bt):
        for h in range(B_HEADS):
            ac, kc, qc = column(a, b, h), column(k, b, h), column(q, b, h)
            for half in range(B_DV // LANES):
                ls = slice(half * LANES, (half + 1) * LANES)
                vh = v[b:b + 1, h * B_DV + half * LANES:h * B_DV + (half + 1) * LANES]
                sn = ac * s_ref[0, b, h, :, ls] + kc * vh
                so_ref[0, b, h, :, ls] = sn
                oraw[b:b + 1, h * B_DV + half * LANES:h * B_DV + (half + 1) * LANES] = jnp.sum(
                    qc * sn, axis=0, keepdims=True)
    gn = gn_ref[...]
    for h in range(B_HEADS):
        vs = slice(h * B_DV, (h + 1) * B_DV)
        o_ref[:, vs] = _rms(oraw[:, vs], gn) * r_ref[:, vs].astype(F32)


def _sgla(qb, kb, vb, la, rb, gn, state, new_state, layer, bt=8):
    nb = qb.shape[0]
    row = lambda i: (i, 0)
    sblk = (1, bt, B_HEADS, B_DK, B_DV)
    of_layer = lambda i: (layer, i, 0, 0, 0)
    chain_specs, chain_args, aliases = [], [], {}
    if new_state is not None:
        chain_specs, chain_args, aliases = [pl.BlockSpec(memory_space=pl.ANY)], [new_state], {7: 1}
    return pl.pallas_call(
        functools.partial(_sgla_kernel, bt=bt),
        grid=(nb // bt,),
        input_output_aliases=aliases,
        in_specs=[
            pl.BlockSpec((bt, B_KW), row),
            pl.BlockSpec((bt, B_KW), row),
            pl.BlockSpec((bt, B_VW), row),
            pl.BlockSpec((bt, B_KW), row),
            pl.BlockSpec((bt, B_VW), row),
            pl.BlockSpec((1, B_DV), lambda i: (0, 0)),
            pl.BlockSpec(sblk, of_layer),
        ] + chain_specs,
        out_specs=[pl.BlockSpec((bt, B_VW), row), pl.BlockSpec(sblk, of_layer)],
        out_shape=[jax.ShapeDtypeStruct((nb, B_VW), F32), jax.ShapeDtypeStruct(state.shape, F32)],
        scratch_shapes=[pltpu.VMEM((bt, B_VW), F32)],
        compiler_params=_params(("parallel",)),
        name="gla_sample",
    )(qb, kb, vb, la, rb, gn, state, *chain_args)


def _rope_tables(pos):
    half = A_ROT_DIM // 2
    inv_freq = 1.0 / (ROPE_THETA ** (jnp.arange(half, dtype=F32) / half))
    ang = pos.astype(F32)[:, None] * inv_freq[None, :]
    cos, sin = jnp.cos(ang), jnp.sin(ang)
    n = pos.shape[0]
    pad = A_HEAD_DIM - A_ROT_DIM
    cos_h = jnp.concatenate([cos, cos, jnp.ones((n, pad), F32)], axis=1)
    sa_h = jnp.concatenate([-sin, jnp.zeros((n, half + pad), F32)], axis=1)
    sb_h = jnp.concatenate([jnp.zeros((n, half), F32), sin, jnp.zeros((n, pad), F32)], axis=1)
    rep = LANES // A_HEAD_DIM
    return jnp.tile(cos_h, (1, rep)), jnp.tile(sa_h, (1, rep)), jnp.tile(sb_h, (1, rep))


def _kv_rows(k, v):
    k = k.astype(F32).reshape(k.shape[:-1] + (A_HEADS, A_HEAD_DIM))
    v = v.astype(F32).reshape(v.shape[:-1] + (A_HEADS, A_HEAD_DIM))
    return jnp.stack([k, v], axis=-3)


def _last_tokens(x, keep):
    bsz, dil, n, w = x.shape
    rows = keep // dil
    return jnp.swapaxes(x[:, :, n - rows:, :], 1, 2).reshape(bsz, keep, w)


def kernel(x_prompt, x_sample, cache_a1_kv, cache_a2_kv, cache_a3_kv, state_gla, norm1_g, w_in, w_gate_up, b_gate,
           gla_norm_g, w_pa, w_pb, w_o, norm2_g, w_ff1, w_ff2, final_norm_g):
    bsz, t, _ = x_prompt.shape
    nb = x_sample.shape[0]
    depth = w_in.shape[0]
    assert x_sample.shape[1] == 1 and t % A_TOKENS == 0 and t % GLA_CHUNK == 0
    caches = (cache_a1_kv, cache_a2_kv, cache_a3_kv)
    for c, (window, _) in zip(caches, A_GROUPS):
        assert c.shape[2] == window

    w_in_p = _win_prep(w_in)
    wgu_p = jnp.concatenate([w_gate_up, jnp.zeros((depth, GL_PAD - B_GATE_RANK, B_KW), F32)], axis=1)
    wgu_hi = wgu_p.astype(BF16)
    wgu_lo = (wgu_p - wgu_hi.astype(F32)).astype(BF16)
    w_pa_b, w_pb_b, w_o_b = w_pa.astype(BF16), w_pb.astype(BF16), w_o.astype(BF16)
    w_ff1_b, w_ff2_b = w_ff1.astype(BF16), w_ff2.astype(BF16)

    tm_p = 512
    tabs_p = _rope_tables(jnp.arange(t, dtype=jnp.int32))
    tabs_s = _rope_tables(jnp.full((nb,), PAST_LEN, dtype=jnp.int32))

    xp = x_prompt.reshape(bsz * t, D_MODEL)
    xs = x_sample.reshape(nb, D_MODEL)
    p_kv = [[], [], []]
    s_kv = [[], [], []]
    p_st = []
    s_st = None
    fg = final_norm_g[None]
    for l in range(depth):
        g1 = norm1_g[l][None]
        g2 = norm2_g[l][None]
        bg = b_gate[l][None]
        gn = gla_norm_g[l][None]
        last = l == depth - 1

        outs = _inproj(xp, g1, w_in_p, tabs_p, wgu_hi, wgu_lo, bg, l, tm_p, seq_len=t)
        qkv, (qb, kb, vb, la, rb, ga, gb) = outs[:3 * N_GROUPS], outs[3 * N_GROUPS:]
        oa = _attn_prompt(qkv, bsz, t)
        ob, st_p = _gla_prompt(qb, kb, vb, la, rb, gn, bsz, t)
        xp, h2 = _merge(oa, ob, ga, gb, xp, w_pa_b, w_pb_b, w_o_b, g2, l, tm_p)
        xp = _ffn(h2, xp, w_ff1_b, w_ff2_b, fg, l, 1024, last)
        for g, (window, _) in enumerate(A_GROUPS):
            keep = min(window, t)
            p_kv[g].append(_kv_rows(_last_tokens(qkv[N_GROUPS + g], keep), _last_tokens(qkv[2 * N_GROUPS + g], keep)))
        p_st.append(st_p)

        qa, ka, va, qb, kb, vb, la, rb, ga, gb = _inproj(xs, g1, w_in_p, tabs_s, wgu_hi, wgu_lo, bg, l, nb)
        oa = _sattn(qa, ka, va, caches, l)
        ob, s_st = _sgla(qb, kb, vb, la, rb, gn, state_gla, s_st, l)
        xs, h2 = _merge(oa, ob, ga, gb, xs, w_pa_b, w_pb_b, w_o_b, g2, l, nb)
        xs = _ffn(h2, xs, w_ff1_b, w_ff2_b, fg, l, nb, last)
        for g in range(N_GROUPS):
            gs = slice(g * A_OUT, (g + 1) * A_OUT)
            s_kv[g].append(_kv_rows(ka[:, None, gs], va[:, None, gs]))

    y_prompt = xp.reshape(bsz, t, D_MODEL)
    y_sample = xs.reshape(nb, 1, D_MODEL)
    return (y_prompt, y_sample,
            jnp.stack(p_kv[0]), jnp.stack(p_kv[1]), jnp.stack(p_kv[2]), jnp.stack(p_st),
            jnp.stack(s_kv[0]), jnp.stack(s_kv[1]), jnp.stack(s_kv[2]), s_st)
```

```python
import functools

import jax
import jax.numpy as jnp
from jax import lax
from jax.experimental import pallas as pl
from jax.experimental.pallas import tpu as pltpu

F32 = jnp.float32
BF16 = jnp.bfloat16

D_MODEL = 1024
PAST_LEN = 2048
A_GROUPS = ((128, 1), (512, 4), (2048, 16))
N_GROUPS = 3
A_HEADS = 4
A_HEAD_DIM = 64
A_ROT_DIM = A_HEAD_DIM // 4
ROPE_THETA = 500000.0
A_QKV = N_GROUPS * A_HEADS * A_HEAD_DIM
A_OUT = A_HEADS * A_HEAD_DIM
A_SPAN = 128
A_TOKENS = A_GROUPS[-1][1] * A_SPAN

B_HEADS = 4
B_DK = 128
B_DV = 256
B_KW = B_HEADS * B_DK
B_VW = B_HEADS * B_DV
B_GATE_RANK = 16
B_GATE_TAU = 16.0
D_FF = 4 * D_MODEL
EPS = 1e-6

LANES = 128
GLA_CHUNK = 128
GLA_SUB = 16
TM_DENSE = 512
TM_FFN = 1024
TF_FFN = 2048
SATTN_BT = 4
SGLA_BT = 8
ATTN_UNROLL_FIRST = 4
ATTN_UNROLL_LATER = 5
NEG_BIG = -1e30
VMEM_LIMIT = 56 * 1024 * 1024

GL_PAD = LANES
_SEG_SIZES = (A_QKV, A_QKV, A_QKV, B_KW, B_KW, B_VW, GL_PAD, B_VW, D_MODEL, D_MODEL)
_SEG_NAMES = ("qa", "ka", "va", "qb", "kb", "vb", "gl", "rb", "ga", "gb")
_SEG = {}
_off = 0
for _n, _s in zip(_SEG_NAMES, _SEG_SIZES):
    _SEG[_n] = (_off, _off + _s)
    _off += _s
IN_WIDTH_PAD = _off

NT_DIMS = (((1,), (1,)), ((), ()))
TN_DIMS = (((0,), (0,)), ((), ()))


def _params(sem):
    return pltpu.CompilerParams(dimension_semantics=sem, vmem_limit_bytes=VMEM_LIMIT)


def _sigmoid(x):
    return 1.0 / (1.0 + jnp.exp(-x))


def _log_sigmoid(x):
    return jnp.minimum(x, 0.0) - jnp.log(1.0 + jnp.exp(-jnp.abs(x)))


def _rms(x, g):
    ms = jnp.mean(x * x, axis=-1, keepdims=True)
    return x * lax.rsqrt(ms + EPS) * g


def _split_bf16(x):
    hi = x.astype(BF16)
    return hi, (x - hi.astype(F32)).astype(BF16)


def _win_prep_kernel(w_ref, o_ref):
    gl0 = _SEG["gl"][0]
    shift = GL_PAD - B_GATE_RANK
    o_ref[:, 0:gl0] = w_ref[:, 0:gl0].astype(o_ref.dtype)
    lane = lax.broadcasted_iota(jnp.int32, (w_ref.shape[0], LANES), 1)
    o_ref[:, gl0:gl0 + LANES] = jnp.where(lane < B_GATE_RANK, w_ref[:, gl0:gl0 + LANES], 0.0).astype(o_ref.dtype)
    for j in range((IN_WIDTH_PAD - gl0 - LANES) // LANES):
        a = gl0 + j * LANES
        tile = jnp.concatenate([w_ref[:, a:a + LANES][:, B_GATE_RANK:], w_ref[:, a + LANES:a + LANES + B_GATE_RANK]],
                               axis=1)
        o_ref[:, a + shift + B_GATE_RANK:a + shift + B_GATE_RANK + LANES] = tile.astype(o_ref.dtype)


def _win_prep(w_in, tr=256):
    depth, d, width = w_in.shape
    assert width + GL_PAD - B_GATE_RANK == IN_WIDTH_PAD
    return pl.pallas_call(
        _win_prep_kernel,
        grid=(depth, d // tr),
        in_specs=[pl.BlockSpec((None, tr, width), lambda l, i: (l, i, 0))],
        out_specs=pl.BlockSpec((None, tr, IN_WIDTH_PAD), lambda l, i: (l, i, 0)),
        out_shape=jax.ShapeDtypeStruct((depth, d, IN_WIDTH_PAD), BF16),
        compiler_params=_params(("parallel", "parallel")),
        name="win_prep",
    )(w_in)


def _inproj_kernel(x_ref, g_ref, w_ref, cos_ref, sa_ref, sb_ref, wgh_ref, wgl_ref, bg_ref, *refs, dilate):
    n_attn = 3 * N_GROUPS if dilate else 3
    attn_refs = refs[:n_attn]
    qb_ref, kb_ref, vb_ref, la_ref, rb_ref, ga_ref, gb_ref = refs[n_attn:n_attn + 7]
    scratch = refs[n_attn + 7:]
    tm = x_ref.shape[0]
    h = _rms(x_ref[...], g_ref[...]).astype(BF16)

    def seg(name):
        c0, c1 = _SEG[name]
        return jnp.dot(h, w_ref[:, c0:c1], preferred_element_type=F32)

    cosl, sa, sb = cos_ref[...], sa_ref[...], sb_ref[...]

    def rope(xc):
        return xc * cosl + pltpu.roll(xc, LANES - A_ROT_DIM // 2, 1) * sa + pltpu.roll(xc, A_ROT_DIM // 2, 1) * sb

    def attn_store(name, which, scale, rotate):
        acc = seg(name)
        tiles_per_group = A_OUT // LANES
        for g, (_, dil) in enumerate(A_GROUPS):
            for c in range(tiles_per_group):
                cg = g * tiles_per_group + c
                xc = acc[:, cg * LANES:(cg + 1) * LANES]
                if rotate:
                    xc = rope(xc)
                if scale != 1.0:
                    xc = xc * scale
                if not dilate:
                    attn_refs[which][:, cg * LANES:(cg + 1) * LANES] = xc.astype(attn_refs[which].dtype)
                    continue
                out_ref = attn_refs[which * N_GROUPS + g]
                ls = slice(c * LANES, (c + 1) * LANES)
                if dil == 1:
                    out_ref[0, 0, :, ls] = xc.astype(out_ref.dtype)
                else:
                    dscr = scratch[0]
                    dscr[...] = xc
                    for r in range(dil):
                        out_ref[0, r, :, ls] = dscr[pl.ds(r, tm // dil, stride=dil), :].astype(out_ref.dtype)

    attn_store("qa", 0, A_HEAD_DIM ** -0.5, True)
    attn_store("ka", 1, 1.0, True)
    attn_store("va", 2, 1.0, False)
    qb_ref[...] = (seg("qb") * (B_DK ** -0.5)).astype(qb_ref.dtype)
    kb_ref[...] = seg("kb").astype(kb_ref.dtype)
    vb_ref[...] = seg("vb").astype(vb_ref.dtype)
    gh, gl = _split_bf16(seg("gl"))
    wgh, wgl = wgh_ref[...], wgl_ref[...]
    xg = (jnp.dot(gh, wgh, preferred_element_type=F32) + jnp.dot(gl, wgh, preferred_element_type=F32)
          + jnp.dot(gh, wgl, preferred_element_type=F32) + bg_ref[...])
    la_ref[...] = _log_sigmoid(xg) * (1.0 / B_GATE_TAU)
    r = seg("rb")
    rb_ref[...] = (r * _sigmoid(r)).astype(rb_ref.dtype)
    ga_ref[...] = _sigmoid(seg("ga")).astype(ga_ref.dtype)
    gb_ref[...] = _sigmoid(seg("gb")).astype(gb_ref.dtype)


def _inproj(x, g1, w, tabs, wgh, wgl, bg, layer, tm, seq_len=None):
    n = x.shape[0]
    dilate = seq_len is not None
    row = lambda i: (i, 0)
    const = lambda i: (0, 0)
    of_layer = lambda i: (layer, 0, 0)
    if dilate:
        tpb = seq_len // tm
        tab = lambda i: (i % tpb, 0)
        attn_shapes, attn_specs = [], []
        for _ in range(3):
            for _, dil in A_GROUPS:
                attn_shapes.append(jax.ShapeDtypeStruct((n // seq_len, dil, seq_len // dil, A_OUT), BF16))
                attn_specs.append(pl.BlockSpec((1, dil, tm // dil, A_OUT), lambda i: (i // tpb, 0, i % tpb, 0)))
        scratch = [pltpu.VMEM((tm, LANES), F32)]
    else:
        tab = row
        attn_shapes = [jax.ShapeDtypeStruct((n, A_QKV), F32)] * 3
        attn_specs = [pl.BlockSpec((tm, A_QKV), row)] * 3
        scratch = []
    rest = (("qb", B_KW, BF16), ("kb", B_KW, BF16), ("vb", B_VW, BF16), ("la", B_KW, F32),
            ("rb", B_VW, BF16), ("ga", D_MODEL, BF16), ("gb", D_MODEL, BF16))
    out_shape = attn_shapes + [jax.ShapeDtypeStruct((n, wd), dt) for _, wd, dt in rest]
    out_specs = attn_specs + [pl.BlockSpec((tm, wd), row) for _, wd, _ in rest]
    return pl.pallas_call(
        functools.partial(_inproj_kernel, dilate=dilate),
        grid=(n // tm,),
        in_specs=[
            pl.BlockSpec((tm, D_MODEL), row),
            pl.BlockSpec((1, D_MODEL), const),
            pl.BlockSpec((None, D_MODEL, IN_WIDTH_PAD), of_layer, pipeline_mode=pl.Buffered(1)),
            pl.BlockSpec((tm, LANES), tab),
            pl.BlockSpec((tm, LANES), tab),
            pl.BlockSpec((tm, LANES), tab),
            pl.BlockSpec((None, GL_PAD, B_KW), of_layer),
            pl.BlockSpec((None, GL_PAD, B_KW), of_layer),
            pl.BlockSpec((1, B_KW), const),
        ],
        out_specs=out_specs,
        out_shape=out_shape,
        scratch_shapes=scratch,
        compiler_params=_params(("parallel",)),
        name="inproj_prompt" if dilate else "inproj_sample",
    )(x, g1, w, *tabs, wgh, wgl, bg)


def _attn_pair(qp, kw, vw, mask2, low):
    zero = jnp.zeros_like(qp)
    q2 = jnp.concatenate([jnp.where(low, qp, zero), jnp.where(low, zero, qp)], axis=0)
    s = lax.dot_general(q2, kw, NT_DIMS, preferred_element_type=F32)
    s = jnp.where(mask2, s, NEG_BIG)
    m = jnp.max(s, axis=1, keepdims=True)
    p = jnp.exp(s - m)
    l = jnp.sum(p, axis=1, keepdims=True)
    pv = jnp.dot(p.astype(BF16), vw, preferred_element_type=F32)
    o2 = pv / l
    lse2 = m + jnp.log(l)
    o = jnp.where(low, o2[0:A_SPAN], o2[A_SPAN:])
    lse = jnp.where(low, jnp.broadcast_to(lse2[0:A_SPAN], (A_SPAN, LANES)),
                    jnp.broadcast_to(lse2[A_SPAN:], (A_SPAN, LANES)))
    return o, lse


def _attn_kernel(*refs):
    ins = refs[:5 * N_GROUPS]
    o_ref = refs[5 * N_GROUPS]
    mx, sx, nx, ro, rl = refs[5 * N_GROUPS + 1:]
    first = pl.program_id(1) == 0
    row = lax.broadcasted_iota(jnp.int32, (2 * A_SPAN, 2 * A_SPAN), 0) & (A_SPAN - 1)
    col = lax.broadcasted_iota(jnp.int32, (2 * A_SPAN, 2 * A_SPAN), 1)
    band = (col >= row) & (col <= row + A_SPAN)
    band_first = band & (col >= jnp.where(first, A_SPAN, 0))
    low = lax.broadcasted_iota(jnp.int32, (A_SPAN, LANES), 1) < A_HEAD_DIM
    pairs = A_HEADS // 2

    for g, (_, dil) in enumerate(A_GROUPS):
        q_ref, kp_ref, kc_ref, vp_ref, vc_ref = ins[5 * g:5 * g + 5]
        nrow = A_TOKENS // dil
        nj = nrow // A_SPAN

        def emit(r, j, o, lse, hp):
            r0 = pl.multiple_of(j * A_SPAN, A_SPAN)
            if dil == 1:
                mx[hp, pl.ds(r0, A_SPAN), :] = lse
                sx[hp, pl.ds(r0, A_SPAN), :] = jnp.ones_like(lse)
                nx[hp, pl.ds(r0, A_SPAN), :] = o
            else:
                base = pl.multiple_of(r * nrow + r0, A_SPAN)
                ro[hp, pl.ds(base, A_SPAN), :] = o
                rl[hp, pl.ds(base, A_SPAN), :] = lse

        def first_block(r, carry):
            for hp in range(pairs):
                cs = slice(hp * LANES, (hp + 1) * LANES)
                kw = jnp.concatenate([kp_ref[0, r, :, cs], kc_ref[0, r, 0:A_SPAN, cs]], axis=0)
                vw = jnp.concatenate([vp_ref[0, r, :, cs], vc_ref[0, r, 0:A_SPAN, cs]], axis=0)
                o, lse = _attn_pair(q_ref[0, r, 0:A_SPAN, cs], kw, vw, band_first, low)
                emit(r, 0, o, lse, hp)
            return carry

        lax.fori_loop(0, dil, first_block, 0, unroll=min(dil, ATTN_UNROLL_FIRST))

        if nj > 1:
            def later_blocks(r, carry):
                def later_block(j, inner):
                    k0 = pl.multiple_of((j - 1) * A_SPAN, A_SPAN)
                    q0 = pl.multiple_of(j * A_SPAN, A_SPAN)
                    for hp in range(pairs):
                        cs = slice(hp * LANES, (hp + 1) * LANES)
                        o, lse = _attn_pair(q_ref[0, r, pl.ds(q0, A_SPAN), cs],
                                            kc_ref[0, r, pl.ds(k0, 2 * A_SPAN), cs],
                                            vc_ref[0, r, pl.ds(k0, 2 * A_SPAN), cs], band, low)
                        emit(r, j, o, lse, hp)
                    return inner

                return lax.fori_loop(1, nj, later_block, carry, unroll=min(nj - 1, ATTN_UNROLL_LATER))

            lax.fori_loop(0, dil, later_blocks, 0)

        if dil > 1:
            for hp in range(pairs):
                for r in range(dil):
                    rows = pl.ds(r, nrow, stride=dil)
                    lse = rl[hp, r * nrow:(r + 1) * nrow, :]
                    m_old = mx[hp, rows, :]
                    m_new = jnp.maximum(m_old, lse)
                    a_old = jnp.exp(m_old - m_new)
                    a_new = jnp.exp(lse - m_new)
                    mx[hp, rows, :] = m_new
                    sx[hp, rows, :] = sx[hp, rows, :] * a_old + a_new
                    nx[hp, rows, :] = nx[hp, rows, :] * a_old + a_new * ro[hp, r * nrow:(r + 1) * nrow, :]

    for hp in range(pairs):
        o_ref[0, :, hp * LANES:(hp + 1) * LANES] = (nx[hp] / sx[hp]).astype(o_ref.dtype)


def _attn_prompt(qkv, bsz, t):
    in_specs, args = [], []
    for g, (_, dil) in enumerate(A_GROUPS):
        nrow = A_TOKENS // dil
        sub = nrow // A_SPAN
        cur = pl.BlockSpec((1, dil, nrow, A_OUT), lambda b, i: (b, 0, i, 0))
        prev = pl.BlockSpec((1, dil, A_SPAN, A_OUT), lambda b, i, sub=sub: (b, 0, jnp.maximum(i * sub - 1, 0), 0))
        q, k, v = qkv[g], qkv[N_GROUPS + g], qkv[2 * N_GROUPS + g]
        in_specs += [cur, prev, cur, prev, cur]
        args += [q, k, k, v, v]
    o = pl.pallas_call(
        _attn_kernel,
        grid=(bsz, t // A_TOKENS),
        in_specs=in_specs,
        out_specs=pl.BlockSpec((1, A_TOKENS, A_OUT), lambda b, i: (b, i, 0)),
        out_shape=jax.ShapeDtypeStruct((bsz, t, A_OUT), BF16),
        scratch_shapes=[pltpu.VMEM((A_OUT // LANES, A_TOKENS, LANES), F32)] * 5,
        compiler_params=_params(("parallel", "parallel")),
        name="attn_prompt",
    )(*args)
    return o.reshape(bsz * t, A_OUT)


def _gla_kernel(q_ref, k_ref, v_ref, la_ref, r_ref, gn_ref, o_ref, sfin_ref, st_ref):
    c = pl.program_id(0)
    C = GLA_CHUNK
    nsub = C // GLA_SUB
    nbatch = q_ref.shape[0]

    @pl.when(c == 0)
    def _():
        st_ref[...] = jnp.zeros_like(st_ref)

    ri = lax.broadcasted_iota(jnp.int32, (C, C), 0)
    ci = lax.broadcasted_iota(jnp.int32, (C, C), 1)
    tril = ci <= ri
    tril_b = jnp.where(tril, 1.0, 0.0).astype(BF16)
    gn = gn_ref[...]

    for bb in range(nbatch):
        la_hi, la_lo = _split_bf16(la_ref[bb])
        b_all = (jnp.dot(tril_b, la_hi, preferred_element_type=F32)
                 + jnp.dot(tril_b, la_lo, preferred_element_type=F32))
        for h in range(B_HEADS):
            ks = slice(h * B_DK, (h + 1) * B_DK)
            vs = slice(h * B_DV, (h + 1) * B_DV)
            si = bb * B_HEADS + h
            b = b_all[:, ks]
            q = q_ref[bb, :, ks].astype(F32)
            k = k_ref[bb, :, ks].astype(F32)
            v = v_ref[bb, :, vs]
            st = st_ref[si]
            o_inter = lax.dot_general((q * jnp.exp(b)).astype(BF16), st.astype(BF16), NT_DIMS,
                                      preferred_element_type=F32)
            beta = b.reshape(nsub, GLA_SUB, B_DK)[:, 0:1, :]
            beta_rows = jnp.broadcast_to(beta, (nsub, GLA_SUB, B_DK)).reshape(C, B_DK)
            qt = (q * jnp.exp(b - beta_rows)).astype(BF16)
            att_rows = []
            for i in range(nsub):
                n = GLA_SUB * (i + 1)
                bi = b[GLA_SUB * i:GLA_SUB * i + 1, :]
                kt = (k[0:n] * jnp.exp(bi - b[0:n])).astype(BF16)
                if n < C:
                    kt = jnp.concatenate([kt, jnp.zeros((C - n, B_DK), BF16)], axis=0)
                att_rows.append(lax.dot_general(qt[GLA_SUB * i:n], kt, NT_DIMS, preferred_element_type=F32))
            att = jnp.where(tril, jnp.concatenate(att_rows, axis=0), 0.0).astype(BF16)
            o = o_inter + jnp.dot(att, v, preferred_element_type=F32)
            bl = b[C - 1:C, :]
            kh = (k * jnp.exp(bl - b)).astype(BF16)
            st_ref[si] = st * jnp.exp(bl) + lax.dot_general(v, kh, TN_DIMS, preferred_element_type=F32)
            y = _rms(o, gn) * r_ref[bb, :, vs].astype(F32)
            o_ref[bb, :, vs] = y.astype(o_ref.dtype)

    @pl.when(c == pl.num_programs(0) - 1)
    def _():
        for bb in range(nbatch):
            for h in range(B_HEADS):
                sfin_ref[bb, h] = st_ref[bb * B_HEADS + h].T


def _gla_prompt(qb, kb, vb, la, rb, gn, bsz, t):
    C = GLA_CHUNK
    blk = lambda c: (0, c, 0)
    o, sfin = pl.pallas_call(
        _gla_kernel,
        grid=(t // C,),
        in_specs=[
            pl.BlockSpec((bsz, C, B_KW), blk),
            pl.BlockSpec((bsz, C, B_KW), blk),
            pl.BlockSpec((bsz, C, B_VW), blk),
            pl.BlockSpec((bsz, C, B_KW), blk),
            pl.BlockSpec((bsz, C, B_VW), blk),
            pl.BlockSpec((1, B_DV), lambda c: (0, 0)),
        ],
        out_specs=[pl.BlockSpec((bsz, C, B_VW), blk),
                   pl.BlockSpec((bsz, B_HEADS, B_DK, B_DV), lambda c: (0, 0, 0, 0))],
        out_shape=[jax.ShapeDtypeStruct((bsz, t, B_VW), BF16),
                   jax.ShapeDtypeStruct((bsz, B_HEADS, B_DK, B_DV), F32)],
        scratch_shapes=[pltpu.VMEM((bsz * B_HEADS, B_DV, B_DK), F32)],
        compiler_params=_params(("arbitrary",)),
        name="gla_prompt",
    )(qb.reshape(bsz, t, B_KW), kb.reshape(bsz, t, B_KW), vb.reshape(bsz, t, B_VW),
      la.reshape(bsz, t, B_KW), rb.reshape(bsz, t, B_VW), gn)
    return o.reshape(bsz * t, B_VW), sfin


def _merge_kernel(oa_ref, ob_ref, ga_ref, gb_ref, x_ref, wpa_ref, wpb_ref, wo_ref, g2_ref, xo_ref, h2_ref):
    ya = jnp.dot(oa_ref[...].astype(BF16), wpa_ref[...], preferred_element_type=F32)
    yb = jnp.dot(ob_ref[...].astype(BF16), wpb_ref[...], preferred_element_type=F32)
    mixed = ga_ref[...].astype(F32) * ya + gb_ref[...].astype(F32) * yb
    xn = x_ref[...] + jnp.dot(mixed.astype(BF16), wo_ref[...], preferred_element_type=F32)
    xo_ref[...] = xn
    h2_ref[...] = _rms(xn, g2_ref[...]).astype(h2_ref.dtype)


def _merge(oa, ob, ga, gb, x, wpa, wpb, wo, g2, layer, tm):
    n = x.shape[0]
    row = lambda i: (i, 0)
    const = lambda i: (0, 0)
    of_layer = lambda i: (layer, 0, 0)
    return pl.pallas_call(
        _merge_kernel,
        grid=(n // tm,),
        in_specs=[
            pl.BlockSpec((tm, A_OUT), row),
            pl.BlockSpec((tm, B_VW), row),
            pl.BlockSpec((tm, D_MODEL), row),
            pl.BlockSpec((tm, D_MODEL), row),
            pl.BlockSpec((tm, D_MODEL), row),
            pl.BlockSpec((None, A_OUT, D_MODEL), of_layer),
            pl.BlockSpec((None, B_VW, D_MODEL), of_layer),
            pl.BlockSpec((None, D_MODEL, D_MODEL), of_layer),
            pl.BlockSpec((1, D_MODEL), const),
        ],
        out_specs=[pl.BlockSpec((tm, D_MODEL), row), pl.BlockSpec((tm, D_MODEL), row)],
        out_shape=[jax.ShapeDtypeStruct((n, D_MODEL), F32), jax.ShapeDtypeStruct((n, D_MODEL), BF16)],
        compiler_params=_params(("parallel",)),
        name="merge",
    )(oa, ob, ga, gb, x, wpa, wpb, wo, g2)


def _ffn_kernel(h_ref, x_ref, w1_ref, w2_ref, gf_ref, o_ref, *, final_norm):
    k = pl.program_id(1)

    @pl.when(k == 0)
    def _():
        o_ref[...] = x_ref[...]

    a = jnp.maximum(jnp.dot(h_ref[...], w1_ref[...], preferred_element_type=F32), 0.0)
    o_ref[...] += jnp.dot((a * a).astype(BF16), w2_ref[...], preferred_element_type=F32)

    if final_norm:
        @pl.when(k == pl.num_programs(1) - 1)
        def _():
            o_ref[...] = _rms(o_ref[...], gf_ref[...])


def _ffn(h2, x, w1, w2, gf, layer, tm, final_norm, tf=TF_FFN):
    n = x.shape[0]
    return pl.pallas_call(
        functools.partial(_ffn_kernel, final_norm=final_norm),
        grid=(n // tm, D_FF // tf),
        in_specs=[
            pl.BlockSpec((tm, D_MODEL), lambda i, k: (i, 0)),
            pl.BlockSpec((tm, D_MODEL), lambda i, k: (i, 0)),
            pl.BlockSpec((None, D_MODEL, tf), lambda i, k: (layer, 0, k)),
            pl.BlockSpec((None, tf, D_MODEL), lambda i, k: (layer, k, 0)),
            pl.BlockSpec((1, D_MODEL), lambda i, k: (0, 0)),
        ],
        out_specs=pl.BlockSpec((tm, D_MODEL), lambda i, k: (i, 0)),
        out_shape=jax.ShapeDtypeStruct((n, D_MODEL), F32),
        compiler_params=_params(("parallel", "arbitrary")),
        name="ffn",
    )(h2, x, w1, w2, gf)


def _sattn_kernel(q_ref, k_ref, v_ref, c1_ref, c2_ref, c3_ref, o_ref, *, bt):
    rows = 8
    head_of_lane = lax.broadcasted_iota(jnp.int32, (rows, A_OUT), 1) >> (A_HEAD_DIM.bit_length() - 1)
    own = head_of_lane == lax.broadcasted_iota(jnp.int32, (rows, A_OUT), 0)
    for b in range(bt):
        outs, lses = [], []
        tok = pl.ds(pl.program_id(0) * bt + b, 1)
        for g, (c_ref, (_, dil)) in enumerate(zip((c1_ref, c2_ref, c3_ref), A_GROUPS)):
            gs = slice(g * A_OUT, (g + 1) * A_OUT)
            q = jnp.where(own, q_ref[tok, gs], 0.0)
            buf_len = c_ref.shape[-1]
            kc = c_ref[0, b, 0].reshape(A_OUT, buf_len).astype(BF16)
            vc = c_ref[0, b, 1].reshape(A_OUT, buf_len).astype(BF16)
            s = jnp.dot(q.astype(BF16), kc, preferred_element_type=F32)
            pos = lax.broadcasted_iota(jnp.int32, (rows, buf_len), 1)
            s = jnp.where((pos & (dil - 1)) == 0, s, NEG_BIG)
            sn = jnp.sum(q * k_ref[tok, gs], axis=1, keepdims=True)
            m = jnp.maximum(jnp.max(s, axis=1, keepdims=True), sn)
            p = jnp.exp(s - m)
            pn = jnp.exp(sn - m)
            l = jnp.sum(p, axis=1, keepdims=True) + pn
            pv = lax.dot_general(p.astype(BF16), vc, NT_DIMS, preferred_element_type=F32)
            outs.append(jnp.where(own, pv + pn * v_ref[tok, gs], 0.0) / l)
            lses.append(m + jnp.log(l))
        m = jnp.maximum(jnp.maximum(lses[0], lses[1]), lses[2])
        es = [jnp.exp(x - m) for x in lses]
        o = (es[0] * outs[0] + es[1] * outs[1] + es[2] * outs[2]) / (es[0] + es[1] + es[2])
        o_ref[tok, :] = jnp.sum(o, axis=0, keepdims=True)


def _sattn(qa, ka, va, caches, layer, bt=SATTN_BT):
    nb = qa.shape[0]
    views, cspecs = [], []
    for c in caches:
        v = jnp.transpose(c, (0, 1, 3, 4, 5, 2))
        views.append(v)
        cspecs.append(pl.BlockSpec((1, bt) + v.shape[2:], lambda i: (layer, i, 0, 0, 0, 0)))
    whole = lambda i: (0, 0)
    return pl.pallas_call(
        functools.partial(_sattn_kernel, bt=bt),
        grid=(nb // bt,),
        in_specs=[pl.BlockSpec((nb, A_QKV), whole)] * 3 + cspecs,
        out_specs=pl.BlockSpec((nb, A_OUT), whole),
        out_shape=jax.ShapeDtypeStruct((nb, A_OUT), F32),
        compiler_params=_params(("arbitrary",)),
        name="attn_sample",
    )(qa, ka, va, *views)


def _sgla_kernel(q_ref, k_ref, v_ref, la_ref, r_ref, gn_ref, s_ref, carried_ref, o_ref, so_ref, oraw, *, bt):
    del carried_ref
    a = jnp.exp(la_ref[...])
    q = q_ref[...].astype(F32)
    k = k_ref[...].astype(F32)
    v = v_ref[...].astype(F32)

    def column(arr, b, h):
        r = arr[b:b + 1, h * B_DK:(h + 1) * B_DK]
        return jnp.broadcast_to(r, (LANES, B_DK)).T

    for b in range(bt):
        for h in range(B_HEADS):
            ac, kc, qc = column(a, b, h), column(k, b, h), column(q, b, h)
            for half in range(B_DV // LANES):
                ls = slice(half * LANES, (half + 1) * LANES)
                vh = v[b:b + 1, h * B_DV + half * LANES:h * B_DV + (half + 1) * LANES]
                sn = ac * s_ref[0, b, h, :, ls] + kc * vh
                so_ref[0, b, h, :, ls] = sn
                oraw[b:b + 1, h * B_DV + half * LANES:h * B_DV + (half + 1) * LANES] = jnp.sum(
                    qc * sn, axis=0, keepdims=True)
    gn = gn_ref[...]
    for h in range(B_HEADS):
        vs = slice(h * B_DV, (h + 1) * B_DV)
        o_ref[:, vs] = _rms(oraw[:, vs], gn) * r_ref[:, vs].astype(F32)


def _sgla(qb, kb, vb, la, rb, gn, state, new_state, layer, bt=SGLA_BT):
    nb = qb.shape[0]
    row = lambda i: (i, 0)
    sblk = (1, bt, B_HEADS, B_DK, B_DV)
    of_layer = lambda i: (layer, i, 0, 0, 0)
    return pl.pallas_call(
        functools.partial(_sgla_kernel, bt=bt),
        grid=(nb // bt,),
        input_output_aliases={7: 1},
        in_specs=[
            pl.BlockSpec((bt, B_KW), row),
            pl.BlockSpec((bt, B_KW), row),
            pl.BlockSpec((bt, B_VW), row),
            pl.BlockSpec((bt, B_KW), row),
            pl.BlockSpec((bt, B_VW), row),
            pl.BlockSpec((1, B_DV), lambda i: (0, 0)),
            pl.BlockSpec(sblk, of_layer),
            pl.BlockSpec(memory_space=pl.ANY),
        ],
        out_specs=[pl.BlockSpec((bt, B_VW), row), pl.BlockSpec(sblk, of_layer)],
        out_shape=[jax.ShapeDtypeStruct((nb, B_VW), F32), jax.ShapeDtypeStruct(state.shape, F32)],
        scratch_shapes=[pltpu.VMEM((bt, B_VW), F32)],
        compiler_params=_params(("parallel",)),
        name="gla_sample",
    )(qb, kb, vb, la, rb, gn, state, new_state)


def _rope_tables(pos):
    half = A_ROT_DIM // 2
    inv_freq = 1.0 / (ROPE_THETA ** (jnp.arange(half, dtype=F32) / half))
    ang = pos.astype(F32)[:, None] * inv_freq[None, :]
    cos, sin = jnp.cos(ang), jnp.sin(ang)
    n = pos.shape[0]
    pad = A_HEAD_DIM - A_ROT_DIM
    cos_h = jnp.concatenate([cos, cos, jnp.ones((n, pad), F32)], axis=1)
    sa_h = jnp.concatenate([-sin, jnp.zeros((n, half + pad), F32)], axis=1)
    sb_h = jnp.concatenate([jnp.zeros((n, half), F32), sin, jnp.zeros((n, pad), F32)], axis=1)
    rep = LANES // A_HEAD_DIM
    return jnp.tile(cos_h, (1, rep)), jnp.tile(sa_h, (1, rep)), jnp.tile(sb_h, (1, rep))


def _kv_rows(k, v):
    k = k.astype(F32).reshape(k.shape[:-1] + (A_HEADS, A_HEAD_DIM))
    v = v.astype(F32).reshape(v.shape[:-1] + (A_HEADS, A_HEAD_DIM))
    return jnp.stack([k, v], axis=-3)


def _last_tokens(x, keep):
    bsz, dil, n, w = x.shape
    rows = keep // dil
    return jnp.swapaxes(x[:, :, n - rows:, :], 1, 2).reshape(bsz, keep, w)


def kernel(x_prompt, x_sample, cache_a1_kv, cache_a2_kv, cache_a3_kv, state_gla, norm1_g, w_in, w_gate_up, b_gate,
           gla_norm_g, w_pa, w_pb, w_o, norm2_g, w_ff1, w_ff2, final_norm_g):
    bsz, t, _ = x_prompt.shape
    nb = x_sample.shape[0]
    depth = w_in.shape[0]
    assert x_sample.shape[1] == 1 and t % A_TOKENS == 0 and t % GLA_CHUNK == 0
    caches = (cache_a1_kv, cache_a2_kv, cache_a3_kv)
    for c, (window, _) in zip(caches, A_GROUPS):
        assert c.shape[2] == window

    w_in_p = _win_prep(w_in)
    wgu_p = jnp.concatenate([w_gate_up, jnp.zeros((depth, GL_PAD - B_GATE_RANK, B_KW), F32)], axis=1)
    wgu_hi = wgu_p.astype(BF16)
    wgu_lo = (wgu_p - wgu_hi.astype(F32)).astype(BF16)
    w_pa_b, w_pb_b, w_o_b = w_pa.astype(BF16), w_pb.astype(BF16), w_o.astype(BF16)
    w_ff1_b, w_ff2_b = w_ff1.astype(BF16), w_ff2.astype(BF16)

    tm_p = TM_DENSE
    tabs_p = _rope_tables(jnp.arange(t, dtype=jnp.int32))
    tabs_s = _rope_tables(jnp.full((nb,), PAST_LEN, dtype=jnp.int32))

    xp = x_prompt.reshape(bsz * t, D_MODEL)
    xs = x_sample.reshape(nb, D_MODEL)
    p_kv = [[], [], []]
    s_kv = [[], [], []]
    p_st = []
    s_st = jnp.zeros(state_gla.shape, F32)
    fg = final_norm_g[None]
    for l in range(depth):
        g1 = norm1_g[l][None]
        g2 = norm2_g[l][None]
        bg = b_gate[l][None]
        gn = gla_norm_g[l][None]
        last = l == depth - 1

        outs = _inproj(xp, g1, w_in_p, tabs_p, wgu_hi, wgu_lo, bg, l, tm_p, seq_len=t)
        qkv, (qb, kb, vb, la, rb, ga, gb) = outs[:3 * N_GROUPS], outs[3 * N_GROUPS:]
        oa = _attn_prompt(qkv, bsz, t)
        ob, st_p = _gla_prompt(qb, kb, vb, la, rb, gn, bsz, t)
        xp, h2 = _merge(oa, ob, ga, gb, xp, w_pa_b, w_pb_b, w_o_b, g2, l, tm_p)
        xp = _ffn(h2, xp, w_ff1_b, w_ff2_b, fg, l, TM_FFN, last)
        for g, (window, _) in enumerate(A_GROUPS):
            keep = min(window, t)
            p_kv[g].append(_kv_rows(_last_tokens(qkv[N_GROUPS + g], keep), _last_tokens(qkv[2 * N_GROUPS + g], keep)))
        p_st.append(st_p)

        qa, ka, va, qb, kb, vb, la, rb, ga, gb = _inproj(xs, g1, w_in_p, tabs_s, wgu_hi, wgu_lo, bg, l, nb)
        oa = _sattn(qa, ka, va, caches, l)
        ob, s_st = _sgla(qb, kb, vb, la, rb, gn, state_gla, s_st, l)
        xs, h2 = _merge(oa, ob, ga, gb, xs, w_pa_b, w_pb_b, w_o_b, g2, l, nb)
        xs = _ffn(h2, xs, w_ff1_b, w_ff2_b, fg, l, nb, last)
        for g in range(N_GROUPS):
            gs = slice(g * A_OUT, (g + 1) * A_OUT)
            s_kv[g].append(_kv_rows(ka[:, None, gs], va[:, None, gs]))

    y_prompt = xp.reshape(bsz, t, D_MODEL)
    y_sample = xs.reshape(nb, 1, D_MODEL)
    return (y_prompt, y_sample,
            jnp.stack(p_kv[0]), jnp.stack(p_kv[1]), jnp.stack(p_kv[2]), jnp.stack(p_st),
            jnp.stack(s_kv[0]), jnp.stack(s_kv[1]), jnp.stack(s_kv[2]), s_st)
```
